```python
import math
import jax, jax.numpy as jnp
from jax import lax
import numpy as np

D_MODEL = 1024
BATCH = 16
SEQ = 4096
DEPTH = 4

MEM_LEN = 256
CHUNK = 128
Q_BLOCK = 128
A_GROUPS = 4
A_WIDTH = D_MODEL // 2
A_GROUP_DIM = A_WIDTH // A_GROUPS
B_GROUPS = 4
B_WIDTH = D_MODEL // 2
B_GROUP_DIM = B_WIDTH // B_GROUPS
CONV_WIDTH = 31
C_HEADS = 16
C_HEAD_DIM = D_MODEL // C_HEADS
MEM_HEADS = 4
MEM_HEAD_DIM = D_MODEL // MEM_HEADS
D_FF = ((8 * D_MODEL + 3 * 256 - 1) // (3 * 256)) * 256
DEEPNORM_ALPHA = (2.0 * DEPTH) ** 0.25
DEEPNORM_BETA = (8.0 * DEPTH) ** -0.25
LN_EPS = 1e-5
N_EVEN = (DEPTH + 1) // 2
N_ODD = DEPTH // 2

kernel_name = "hybrid_gmlp_conformer_stickbreaking_deepnorm"


def _layer_norm(x, g, b):
    xf = x.astype(jnp.float32)
    mu = jnp.mean(xf, axis=-1, keepdims=True)
    var = jnp.mean(jnp.square(xf - mu), axis=-1, keepdims=True)
    y = (xf - mu) * lax.rsqrt(var + LN_EPS)
    return (y * g.astype(jnp.float32) + b.astype(jnp.float32)).astype(x.dtype)


def _post_norm(x, f, g, b):
    return _layer_norm(DEEPNORM_ALPHA * x + f, g, b)


def _gmlp_chunked(u, v, ln_g, ln_b, w_s, b_s):
    bn, s, _ = u.shape
    u = jax.nn.gelu(u, approximate=False)
    v = jax.nn.gelu(v, approximate=False)
    vg = v.reshape(bn, s // CHUNK, CHUNK, A_GROUPS, A_GROUP_DIM)
    vg = _layer_norm(vg, ln_g.reshape(A_GROUPS, A_GROUP_DIM), ln_b.reshape(A_GROUPS, A_GROUP_DIM))
    causal = jnp.tril(jnp.ones((CHUNK, CHUNK), dtype=bool))
    w = jnp.where(causal[None], w_s, jnp.zeros_like(w_s))
    mixed = jnp.einsum('gts,bcsgd->bctgd', w, vg) + b_s.T[:, :, None]
    return u * mixed.reshape(bn, s, A_WIDTH)


def _conformer_conv(a, gate, conv_w, conv_b, gn_g, gn_b):
    h = a * jax.nn.sigmoid(gate)
    h = lax.conv_general_dilated(
        h, conv_w[:, None, :].astype(h.dtype), window_strides=(1,),
        padding=[(CONV_WIDTH - 1, 0)],
        dimension_numbers=('NWC', 'WIO', 'NWC'),
        feature_group_count=B_WIDTH) + conv_b
    bn, s, _ = h.shape
    hg = _layer_norm(h.reshape(bn, s, B_GROUPS, B_GROUP_DIM),
                     gn_g.reshape(B_GROUPS, B_GROUP_DIM), gn_b.reshape(B_GROUPS, B_GROUP_DIM))
    return jax.nn.silu(hg).reshape(bn, s, B_WIDTH)


def _stick_breaking(q, k, v):
    s_len = q.shape[2]
    scale = C_HEAD_DIM ** -0.5
    outs = []
    for i in range(s_len // Q_BLOCK):
        t0 = i * Q_BLOCK
        kend = t0 + Q_BLOCK
        qb = q[:, :, t0:kend]
        kb = k[:, :, :kend]
        vb = v[:, :, :kend]
        z = jnp.einsum('bhqd,bhkd->bhqk', qb, kb,
                       preferred_element_type=jnp.float32) * scale
        t_pos = t0 + jnp.arange(Q_BLOCK)[:, None]
        s_pos = jnp.arange(kend)[None, :]
        strict = s_pos < t_pos
        log_not = jnp.where(strict, jax.nn.log_sigmoid(-z), 0.0)
        later = lax.cumsum(log_not, axis=log_not.ndim - 1, reverse=True) - log_not
        att = jnp.where(strict, jnp.exp(jax.nn.log_sigmoid(z) + later), 0.0)
        outs.append(jnp.einsum('bhqk,bhkd->bhqd', att.astype(vb.dtype), vb))
    return jnp.concatenate(outs, axis=2)


def _memory_cross_attention(x, mem, wq, wk, wv, wo):
    bn, s, _ = x.shape
    m = mem.shape[1]
    q = (x @ wq).reshape(bn, s, MEM_HEADS, MEM_HEAD_DIM)
    k = (mem @ wk).reshape(bn, m, MEM_HEADS, MEM_HEAD_DIM)
    v = (mem @ wv).reshape(bn, m, MEM_HEADS, MEM_HEAD_DIM)
    sc = jnp.einsum('bqhd,bkhd->bhqk', q, k,
                    preferred_element_type=jnp.float32) * (MEM_HEAD_DIM ** -0.5)
    p = jax.nn.softmax(sc, axis=-1).astype(v.dtype)
    o = jnp.einsum('bhqk,bkhd->bqhd', p, v).reshape(bn, s, D_MODEL)
    return o @ wo


def _swiglu(x, w1, w3, w2):
    return (jax.nn.silu(x @ w1) * (x @ w3)) @ w2


def _fwd_setup_inputs(seed: int = 0) -> dict:
    key = jax.random.key(seed)
    ks = jax.random.split(key, 24)
    f32 = jnp.float32
    nrm = lambda k, shape, sc: jax.random.normal(k, shape, f32) * sc
    d = D_MODEL
    return {
        "x": nrm(ks[0], (BATCH, SEQ, d), 1.0),
        "mem": nrm(ks[1], (BATCH, MEM_LEN, d), 1.0),
        "w_in_ab": nrm(ks[2], (N_EVEN, d, 2 * A_WIDTH + 2 * B_WIDTH), d ** -0.5),
        "gmlp_ln_g": 1.0 + nrm(ks[3], (N_EVEN, A_WIDTH), 0.02),
        "gmlp_ln_b": nrm(ks[4], (N_EVEN, A_WIDTH), 0.02),
        "gmlp_w_s": nrm(ks[5], (N_EVEN, A_GROUPS, CHUNK, CHUNK), CHUNK ** -0.5),
        "gmlp_b_s": 1.0 + nrm(ks[6], (N_EVEN, A_GROUPS, CHUNK), 0.02),
        "conv_w": nrm(ks[7], (N_EVEN, CONV_WIDTH, B_WIDTH), CONV_WIDTH ** -0.5),
        "conv_b": nrm(ks[8], (N_EVEN, B_WIDTH), 0.02),
        "conv_gn_g": 1.0 + nrm(ks[9], (N_EVEN, B_WIDTH), 0.02),
        "conv_gn_b": nrm(ks[10], (N_EVEN, B_WIDTH), 0.02),
        "w_out_ab": nrm(ks[11], (N_EVEN, A_WIDTH + B_WIDTH, d), (A_WIDTH + B_WIDTH) ** -0.5 * DEEPNORM_BETA),
        "w_qkv_c": nrm(ks[12], (N_ODD, d, 3 * d), d ** -0.5),
        "w_out_c": nrm(ks[13], (N_ODD, d, d), d ** -0.5 * DEEPNORM_BETA),
        "mem_wq": nrm(ks[14], (DEPTH, d, d), d ** -0.5),
        "mem_wk": nrm(ks[15], (DEPTH, d, d), d ** -0.5),
        "mem_wv": nrm(ks[16], (DEPTH, d, d), d ** -0.5),
        "mem_wo": nrm(ks[17], (DEPTH, d, d), d ** -0.5 * DEEPNORM_BETA),
        "ffn_w1": nrm(ks[18], (DEPTH, d, D_FF), d ** -0.5),
        "ffn_w3": nrm(ks[19], (DEPTH, d, D_FF), d ** -0.5),
        "ffn_w2": nrm(ks[20], (DEPTH, D_FF, d), D_FF ** -0.5 * DEEPNORM_BETA),
        "ln_g": 1.0 + nrm(ks[21], (DEPTH, 3, d), 0.02),
        "ln_b": nrm(ks[22], (DEPTH, 3, d), 0.02),
    }


def _fwd_reference(x, mem, w_in_ab, gmlp_ln_g, gmlp_ln_b, gmlp_w_s, gmlp_b_s, conv_w, conv_b,
              conv_gn_g, conv_gn_b, w_out_ab, w_qkv_c, w_out_c, mem_wq, mem_wk, mem_wv,
              mem_wo, ffn_w1, ffn_w3, ffn_w2, ln_g, ln_b):
    bn, s, d = x.shape
    for layer in range(DEPTH):
        if layer % 2 == 0:
            e = layer // 2
            h = x @ w_in_ab[e]
            u = h[..., :A_WIDTH]
            v = h[..., A_WIDTH:2 * A_WIDTH]
            a = h[..., 2 * A_WIDTH:2 * A_WIDTH + B_WIDTH]
            gt = h[..., 2 * A_WIDTH + B_WIDTH:]
            ya = _gmlp_chunked(u, v, gmlp_ln_g[e], gmlp_ln_b[e], gmlp_w_s[e], gmlp_b_s[e])
            yb = _conformer_conv(a, gt, conv_w[e], conv_b[e], conv_gn_g[e], conv_gn_b[e])
            mix = jnp.concatenate([ya, yb], axis=-1) @ w_out_ab[e]
        else:
            o = layer // 2
            qkv = (x @ w_qkv_c[o]).reshape(bn, s, 3, C_HEADS, C_HEAD_DIM)
            q = qkv[:, :, 0].transpose(0, 2, 1, 3)
            k = qkv[:, :, 1].transpose(0, 2, 1, 3)
            v = qkv[:, :, 2].transpose(0, 2, 1, 3)
            y = _stick_breaking(q, k, v).transpose(0, 2, 1, 3).reshape(bn, s, d)
            mix = y @ w_out_c[o]
        x = _post_norm(x, mix, ln_g[layer, 0], ln_b[layer, 0])
        cross = _memory_cross_attention(x, mem, mem_wq[layer], mem_wk[layer], mem_wv[layer], mem_wo[layer])
        x = _post_norm(x, cross, ln_g[layer, 1], ln_b[layer, 1])
        x = _post_norm(x, _swiglu(x, ffn_w1[layer], ffn_w3[layer], ffn_w2[layer]), ln_g[layer, 2], ln_b[layer, 2])
    return x


import jax as _jax
import jax.numpy as _jnp

TWIN_FORMAT = 'train_step'
FWD_PARAMS = ['x', 'mem', 'w_in_ab', 'gmlp_ln_g', 'gmlp_ln_b', 'gmlp_w_s', 'gmlp_b_s', 'conv_w', 'conv_b', 'conv_gn_g', 'conv_gn_b', 'w_out_ab', 'w_qkv_c', 'w_out_c', 'mem_wq', 'mem_wk', 'mem_wv', 'mem_wo', 'ffn_w1', 'ffn_w3', 'ffn_w2', 'ln_g', 'ln_b']
TWIN_WEIGHTS = ['w_in_ab', 'gmlp_ln_g', 'gmlp_ln_b', 'gmlp_w_s', 'gmlp_b_s', 'conv_w', 'conv_b', 'conv_gn_g', 'conv_gn_b', 'w_out_ab', 'w_qkv_c', 'w_out_c', 'mem_wq', 'mem_wk', 'mem_wv', 'mem_wo', 'ffn_w1', 'ffn_w3', 'ffn_w2', 'ln_g', 'ln_b']
TWIN_DIFF_INPUT = 'x'
TWIN_INPUTS = ['x', 'mem', 'w_in_ab', 'gmlp_ln_g', 'gmlp_ln_b', 'gmlp_w_s', 'gmlp_b_s', 'conv_w', 'conv_b', 'conv_gn_g', 'conv_gn_b', 'w_out_ab', 'w_qkv_c', 'w_out_c', 'mem_wq', 'mem_wk', 'mem_wv', 'mem_wo', 'ffn_w1', 'ffn_w3', 'ffn_w2', 'ln_g', 'ln_b', 'loss_target', 'm_w_in_ab', 'm_gmlp_ln_g', 'm_gmlp_ln_b', 'm_gmlp_w_s', 'm_gmlp_b_s', 'm_conv_w', 'm_conv_b', 'm_conv_gn_g', 'm_conv_gn_b', 'm_w_out_ab', 'm_w_qkv_c', 'm_w_out_c', 'm_mem_wq', 'm_mem_wk', 'm_mem_wv', 'm_mem_wo', 'm_ffn_w1', 'm_ffn_w3', 'm_ffn_w2', 'm_ln_g', 'm_ln_b', 'v_w_in_ab', 'v_gmlp_ln_g', 'v_gmlp_ln_b', 'v_gmlp_w_s', 'v_gmlp_b_s', 'v_conv_w', 'v_conv_b', 'v_conv_gn_g', 'v_conv_gn_b', 'v_w_out_ab', 'v_w_qkv_c', 'v_w_out_c', 'v_mem_wq', 'v_mem_wk', 'v_mem_wv', 'v_mem_wo', 'v_ffn_w1', 'v_ffn_w3', 'v_ffn_w2', 'v_ln_g', 'v_ln_b']
TWIN_OUTPUTS = ['loss', 'grad_x', 'grad_w_in_ab', 'grad_gmlp_ln_g', 'grad_gmlp_ln_b', 'grad_gmlp_w_s', 'grad_gmlp_b_s', 'grad_conv_w', 'grad_conv_b', 'grad_conv_gn_g', 'grad_conv_gn_b', 'grad_w_out_ab', 'grad_w_qkv_c', 'grad_w_out_c', 'grad_mem_wq', 'grad_mem_wk', 'grad_mem_wv', 'grad_mem_wo', 'grad_ffn_w1', 'grad_ffn_w3', 'grad_ffn_w2', 'grad_ln_g', 'grad_ln_b', 'delta_w_in_ab', 'delta_gmlp_ln_g', 'delta_gmlp_ln_b', 'delta_gmlp_w_s', 'delta_gmlp_b_s', 'delta_conv_w', 'delta_conv_b', 'delta_conv_gn_g', 'delta_conv_gn_b', 'delta_w_out_ab', 'delta_w_qkv_c', 'delta_w_out_c', 'delta_mem_wq', 'delta_mem_wk', 'delta_mem_wv', 'delta_mem_wo', 'delta_ffn_w1', 'delta_ffn_w3', 'delta_ffn_w2', 'delta_ln_g', 'delta_ln_b', 'new_m_w_in_ab', 'new_m_gmlp_ln_g', 'new_m_gmlp_ln_b', 'new_m_gmlp_w_s', 'new_m_gmlp_b_s', 'new_m_conv_w', 'new_m_conv_b', 'new_m_conv_gn_g', 'new_m_conv_gn_b', 'new_m_w_out_ab', 'new_m_w_qkv_c', 'new_m_w_out_c', 'new_m_mem_wq', 'new_m_mem_wk', 'new_m_mem_wv', 'new_m_mem_wo', 'new_m_ffn_w1', 'new_m_ffn_w3', 'new_m_ffn_w2', 'new_m_ln_g', 'new_m_ln_b', 'new_v_w_in_ab', 'new_v_gmlp_ln_g', 'new_v_gmlp_ln_b', 'new_v_gmlp_w_s', 'new_v_gmlp_b_s', 'new_v_conv_w', 'new_v_conv_b', 'new_v_conv_gn_g', 'new_v_conv_gn_b', 'new_v_w_out_ab', 'new_v_w_qkv_c', 'new_v_w_out_c', 'new_v_mem_wq', 'new_v_mem_wk', 'new_v_mem_wv', 'new_v_mem_wo', 'new_v_ffn_w1', 'new_v_ffn_w3', 'new_v_ffn_w2', 'new_v_ln_g', 'new_v_ln_b']
TWIN_LEAF_KINDS = {'loss': 'loss', 'grad_x': 'grad_x', 'grad_w_in_ab': 'grad_w', 'grad_gmlp_ln_g': 'grad_w', 'grad_gmlp_ln_b': 'grad_w', 'grad_gmlp_w_s': 'grad_w', 'grad_gmlp_b_s': 'grad_w', 'grad_conv_w': 'grad_w', 'grad_conv_b': 'grad_w', 'grad_conv_gn_g': 'grad_w', 'grad_conv_gn_b': 'grad_w', 'grad_w_out_ab': 'grad_w', 'grad_w_qkv_c': 'grad_w', 'grad_w_out_c': 'grad_w', 'grad_mem_wq': 'grad_w', 'grad_mem_wk': 'grad_w', 'grad_mem_wv': 'grad_w', 'grad_mem_wo': 'grad_w', 'grad_ffn_w1': 'grad_w', 'grad_ffn_w3': 'grad_w', 'grad_ffn_w2': 'grad_w', 'grad_ln_g': 'grad_w', 'grad_ln_b': 'grad_w', 'delta_w_in_ab': 'delta_w', 'delta_gmlp_ln_g': 'delta_w', 'delta_gmlp_ln_b': 'delta_w', 'delta_gmlp_w_s': 'delta_w', 'delta_gmlp_b_s': 'delta_w', 'delta_conv_w': 'delta_w', 'delta_conv_b': 'delta_w', 'delta_conv_gn_g': 'delta_w', 'delta_conv_gn_b': 'delta_w', 'delta_w_out_ab': 'delta_w', 'delta_w_qkv_c': 'delta_w', 'delta_w_out_c': 'delta_w', 'delta_mem_wq': 'delta_w', 'delta_mem_wk': 'delta_w', 'delta_mem_wv': 'delta_w', 'delta_mem_wo': 'delta_w', 'delta_ffn_w1': 'delta_w', 'delta_ffn_w3': 'delta_w', 'delta_ffn_w2': 'delta_w', 'delta_ln_g': 'delta_w', 'delta_ln_b': 'delta_w', 'new_m_w_in_ab': 'new_m', 'new_m_gmlp_ln_g': 'new_m', 'new_m_gmlp_ln_b': 'new_m', 'new_m_gmlp_w_s': 'new_m', 'new_m_gmlp_b_s': 'new_m', 'new_m_conv_w': 'new_m', 'new_m_conv_b': 'new_m', 'new_m_conv_gn_g': 'new_m', 'new_m_conv_gn_b': 'new_m', 'new_m_w_out_ab': 'new_m', 'new_m_w_qkv_c': 'new_m', 'new_m_w_out_c': 'new_m', 'new_m_mem_wq': 'new_m', 'new_m_mem_wk': 'new_m', 'new_m_mem_wv': 'new_m', 'new_m_mem_wo': 'new_m', 'new_m_ffn_w1': 'new_m', 'new_m_ffn_w3': 'new_m', 'new_m_ffn_w2': 'new_m', 'new_m_ln_g': 'new_m', 'new_m_ln_b': 'new_m', 'new_v_w_in_ab': 'new_v', 'new_v_gmlp_ln_g': 'new_v', 'new_v_gmlp_ln_b': 'new_v', 'new_v_gmlp_w_s': 'new_v', 'new_v_gmlp_b_s': 'new_v', 'new_v_conv_w': 'new_v', 'new_v_conv_b': 'new_v', 'new_v_conv_gn_g': 'new_v', 'new_v_conv_gn_b': 'new_v', 'new_v_w_out_ab': 'new_v', 'new_v_w_qkv_c': 'new_v', 'new_v_w_out_c': 'new_v', 'new_v_mem_wq': 'new_v', 'new_v_mem_wk': 'new_v', 'new_v_mem_wv': 'new_v', 'new_v_mem_wo': 'new_v', 'new_v_ffn_w1': 'new_v', 'new_v_ffn_w3': 'new_v', 'new_v_ffn_w2': 'new_v', 'new_v_ln_g': 'new_v', 'new_v_ln_b': 'new_v'}


def _forward(args):
    return _fwd_reference(*[args[k] for k in FWD_PARAMS])


def _output_shape():
    out = _jax.eval_shape(lambda: _forward(_fwd_setup_inputs(0)))
    return out.shape, out.dtype

N_MICROBATCH = 1
ADAM_LR = 0.001
ADAM_B1 = 0.9
ADAM_B2 = 0.999
ADAM_EPS = 1e-08
ADAM_WD = 0.01
ADAM_STEP = 10
PER_EXAMPLE_BATCH_AXIS = {'x': 0, 'mem': 0, 'loss_target': 0}
SHARED_INPUTS = []
_WEIGHT_DTYPES = {'w_in_ab': _jnp.float32, 'gmlp_ln_g': _jnp.float32, 'gmlp_ln_b': _jnp.float32, 'gmlp_w_s': _jnp.float32, 'gmlp_b_s': _jnp.float32, 'conv_w': _jnp.float32, 'conv_b': _jnp.float32, 'conv_gn_g': _jnp.float32, 'conv_gn_b': _jnp.float32, 'w_out_ab': _jnp.float32, 'w_qkv_c': _jnp.float32, 'w_out_c': _jnp.float32, 'mem_wq': _jnp.float32, 'mem_wk': _jnp.float32, 'mem_wv': _jnp.float32, 'mem_wo': _jnp.float32, 'ffn_w1': _jnp.float32, 'ffn_w3': _jnp.float32, 'ffn_w2': _jnp.float32, 'ln_g': _jnp.float32, 'ln_b': _jnp.float32}
MOMENT_SCALE = {'w_in_ab': 3.839268e-02, 'gmlp_ln_g': 3.253844e-02, 'gmlp_ln_b': 2.895471e-02, 'gmlp_w_s': 2.987073e-02, 'gmlp_b_s': 4.196844e-02, 'conv_w': 4.091580e-02, 'conv_b': 1.320920e-01, 'conv_gn_g': 6.599013e-02, 'conv_gn_b': 8.853349e-02, 'w_out_ab': 1.373873e-01, 'w_qkv_c': 2.916799e-02, 'w_out_c': 1.029260e-01, 'mem_wq': 6.816824e-03, 'mem_wk': 6.835454e-03, 'mem_wv': 8.178561e-03, 'mem_wo': 1.928879e-02, 'ffn_w1': 2.356109e-02, 'ffn_w3': 2.290581e-02, 'ffn_w2': 9.054763e-02, 'ln_g': 1.858510e+01, 'ln_b': 1.874494e+00}


def _to_microbatches(a, axis):
    t = _jnp.moveaxis(a, axis, 0)
    t = t.reshape((N_MICROBATCH, t.shape[0] // N_MICROBATCH) + t.shape[1:])
    return _jnp.moveaxis(t, 1, axis + 1)


def setup_inputs(seed: int = 0) -> dict:
    inp = _fwd_setup_inputs(seed)
    key = _jax.random.fold_in(_jax.random.key(seed), 7919)
    shape, _ = _output_shape()
    out = dict(inp)
    out["loss_target"] = _jax.random.normal(_jax.random.fold_in(key, 0), shape, _jnp.float32)
    for i, name in enumerate(TWIN_WEIGHTS):
        w = inp[name].astype(_jnp.float32)
        if MOMENT_SCALE is None:
            s = _jnp.sqrt(_jnp.mean(_jnp.square(w)) + 1e-30)
        else:
            s = MOMENT_SCALE[name]
        km, kv = _jax.random.split(_jax.random.fold_in(key, i + 1))
        out[name] = w
        out["m_" + name] = s * _jax.random.normal(km, w.shape, _jnp.float32)
        out["v_" + name] = (s * s) * _jax.random.uniform(kv, w.shape, _jnp.float32, 0.5, 1.5)
    if N_MICROBATCH > 1:
        for name, axis in PER_EXAMPLE_BATCH_AXIS.items():
            out[name] = _to_microbatches(out[name], axis)
    return {'x': out['x'], 'mem': out['mem'], 'w_in_ab': out['w_in_ab'], 'gmlp_ln_g': out['gmlp_ln_g'], 'gmlp_ln_b': out['gmlp_ln_b'], 'gmlp_w_s': out['gmlp_w_s'], 'gmlp_b_s': out['gmlp_b_s'], 'conv_w': out['conv_w'], 'conv_b': out['conv_b'], 'conv_gn_g': out['conv_gn_g'], 'conv_gn_b': out['conv_gn_b'], 'w_out_ab': out['w_out_ab'], 'w_qkv_c': out['w_qkv_c'], 'w_out_c': out['w_out_c'], 'mem_wq': out['mem_wq'], 'mem_wk': out['mem_wk'], 'mem_wv': out['mem_wv'], 'mem_wo': out['mem_wo'], 'ffn_w1': out['ffn_w1'], 'ffn_w3': out['ffn_w3'], 'ffn_w2': out['ffn_w2'], 'ln_g': out['ln_g'], 'ln_b': out['ln_b'], 'loss_target': out['loss_target'], 'm_w_in_ab': out['m_w_in_ab'], 'm_gmlp_ln_g': out['m_gmlp_ln_g'], 'm_gmlp_ln_b': out['m_gmlp_ln_b'], 'm_gmlp_w_s': out['m_gmlp_w_s'], 'm_gmlp_b_s': out['m_gmlp_b_s'], 'm_conv_w': out['m_conv_w'], 'm_conv_b': out['m_conv_b'], 'm_conv_gn_g': out['m_conv_gn_g'], 'm_conv_gn_b': out['m_conv_gn_b'], 'm_w_out_ab': out['m_w_out_ab'], 'm_w_qkv_c': out['m_w_qkv_c'], 'm_w_out_c': out['m_w_out_c'], 'm_mem_wq': out['m_mem_wq'], 'm_mem_wk': out['m_mem_wk'], 'm_mem_wv': out['m_mem_wv'], 'm_mem_wo': out['m_mem_wo'], 'm_ffn_w1': out['m_ffn_w1'], 'm_ffn_w3': out['m_ffn_w3'], 'm_ffn_w2': out['m_ffn_w2'], 'm_ln_g': out['m_ln_g'], 'm_ln_b': out['m_ln_b'], 'v_w_in_ab': out['v_w_in_ab'], 'v_gmlp_ln_g': out['v_gmlp_ln_g'], 'v_gmlp_ln_b': out['v_gmlp_ln_b'], 'v_gmlp_w_s': out['v_gmlp_w_s'], 'v_gmlp_b_s': out['v_gmlp_b_s'], 'v_conv_w': out['v_conv_w'], 'v_conv_b': out['v_conv_b'], 'v_conv_gn_g': out['v_conv_gn_g'], 'v_conv_gn_b': out['v_conv_gn_b'], 'v_w_out_ab': out['v_w_out_ab'], 'v_w_qkv_c': out['v_w_qkv_c'], 'v_w_out_c': out['v_w_out_c'], 'v_mem_wq': out['v_mem_wq'], 'v_mem_wk': out['v_mem_wk'], 'v_mem_wv': out['v_mem_wv'], 'v_mem_wo': out['v_mem_wo'], 'v_ffn_w1': out['v_ffn_w1'], 'v_ffn_w3': out['v_ffn_w3'], 'v_ffn_w2': out['v_ffn_w2'], 'v_ln_g': out['v_ln_g'], 'v_ln_b': out['v_ln_b']}


def _loss(weights, diff, rest, loss_target):
    with _jax.named_scope("forward"):
        args = {**rest, TWIN_DIFF_INPUT: diff, **{k: w.astype(_WEIGHT_DTYPES[k]) for k, w in weights.items()}}
        y = _forward(args)
    with _jax.named_scope("loss_head"):
        err = _jnp.square(y.astype(_jnp.float32) - loss_target)
        return 0.5 * _jnp.sum(_jnp.mean(err, axis=-1)) if err.ndim else 0.5 * err


def _adamw(w, g, m, v):
    m = ADAM_B1 * m + (1.0 - ADAM_B1) * g
    v = ADAM_B2 * v + (1.0 - ADAM_B2) * _jnp.square(g)
    m_hat = m / (1.0 - ADAM_B1 ** ADAM_STEP)
    v_hat = v / (1.0 - ADAM_B2 ** ADAM_STEP)
    delta = -ADAM_LR * (m_hat / (_jnp.sqrt(v_hat) + ADAM_EPS) + ADAM_WD * w)
    return delta, m, v


def reference(x, mem, w_in_ab, gmlp_ln_g, gmlp_ln_b, gmlp_w_s, gmlp_b_s, conv_w, conv_b, conv_gn_g, conv_gn_b, w_out_ab, w_qkv_c, w_out_c, mem_wq, mem_wk, mem_wv, mem_wo, ffn_w1, ffn_w3, ffn_w2, ln_g, ln_b, loss_target, m_w_in_ab, m_gmlp_ln_g, m_gmlp_ln_b, m_gmlp_w_s, m_gmlp_b_s, m_conv_w, m_conv_b, m_conv_gn_g, m_conv_gn_b, m_w_out_ab, m_w_qkv_c, m_w_out_c, m_mem_wq, m_mem_wk, m_mem_wv, m_mem_wo, m_ffn_w1, m_ffn_w3, m_ffn_w2, m_ln_g, m_ln_b, v_w_in_ab, v_gmlp_ln_g, v_gmlp_ln_b, v_gmlp_w_s, v_gmlp_b_s, v_conv_w, v_conv_b, v_conv_gn_g, v_conv_gn_b, v_w_out_ab, v_w_qkv_c, v_w_out_c, v_mem_wq, v_mem_wk, v_mem_wv, v_mem_wo, v_ffn_w1, v_ffn_w3, v_ffn_w2, v_ln_g, v_ln_b):
    given = dict(x=x, mem=mem, w_in_ab=w_in_ab, gmlp_ln_g=gmlp_ln_g, gmlp_ln_b=gmlp_ln_b, gmlp_w_s=gmlp_w_s, gmlp_b_s=gmlp_b_s, conv_w=conv_w, conv_b=conv_b, conv_gn_g=conv_gn_g, conv_gn_b=conv_gn_b, w_out_ab=w_out_ab, w_qkv_c=w_qkv_c, w_out_c=w_out_c, mem_wq=mem_wq, mem_wk=mem_wk, mem_wv=mem_wv, mem_wo=mem_wo, ffn_w1=ffn_w1, ffn_w3=ffn_w3, ffn_w2=ffn_w2, ln_g=ln_g, ln_b=ln_b, loss_target=loss_target, m_w_in_ab=m_w_in_ab, m_gmlp_ln_g=m_gmlp_ln_g, m_gmlp_ln_b=m_gmlp_ln_b, m_gmlp_w_s=m_gmlp_w_s, m_gmlp_b_s=m_gmlp_b_s, m_conv_w=m_conv_w, m_conv_b=m_conv_b, m_conv_gn_g=m_conv_gn_g, m_conv_gn_b=m_conv_gn_b, m_w_out_ab=m_w_out_ab, m_w_qkv_c=m_w_qkv_c, m_w_out_c=m_w_out_c, m_mem_wq=m_mem_wq, m_mem_wk=m_mem_wk, m_mem_wv=m_mem_wv, m_mem_wo=m_mem_wo, m_ffn_w1=m_ffn_w1, m_ffn_w3=m_ffn_w3, m_ffn_w2=m_ffn_w2, m_ln_g=m_ln_g, m_ln_b=m_ln_b, v_w_in_ab=v_w_in_ab, v_gmlp_ln_g=v_gmlp_ln_g, v_gmlp_ln_b=v_gmlp_ln_b, v_gmlp_w_s=v_gmlp_w_s, v_gmlp_b_s=v_gmlp_b_s, v_conv_w=v_conv_w, v_conv_b=v_conv_b, v_conv_gn_g=v_conv_gn_g, v_conv_gn_b=v_conv_gn_b, v_w_out_ab=v_w_out_ab, v_w_qkv_c=v_w_qkv_c, v_w_out_c=v_w_out_c, v_mem_wq=v_mem_wq, v_mem_wk=v_mem_wk, v_mem_wv=v_mem_wv, v_mem_wo=v_mem_wo, v_ffn_w1=v_ffn_w1, v_ffn_w3=v_ffn_w3, v_ffn_w2=v_ffn_w2, v_ln_g=v_ln_g, v_ln_b=v_ln_b)
    weights = {n: given[n] for n in TWIN_WEIGHTS}
    shared = {n: given[n] for n in SHARED_INPUTS}
    per_example = {n: given[n] for n in ['x', 'mem']}
    grad_fn = _jax.value_and_grad(_loss, argnums=(0, 1))

    def one_microbatch(ex, loss_target):
        ex = dict(ex)
        diff = ex.pop(TWIN_DIFF_INPUT)
        return grad_fn(weights, diff, {**shared, **ex}, loss_target)

    if N_MICROBATCH == 1:
        loss, (grad_w, grad_x) = one_microbatch(per_example, given["loss_target"])
    else:
        def body(carry, xs):
            loss_sum, grad_sum = carry
            l_k, (gw_k, gx_k) = one_microbatch(xs[0], xs[1])
            with _jax.named_scope("update"):
                return (loss_sum + l_k, _jax.tree.map(_jnp.add, grad_sum, gw_k)), gx_k

        init = (_jnp.zeros((), _jnp.float32), _jax.tree.map(_jnp.zeros_like, weights))
        (loss, grad_w), grad_x = _jax.lax.scan(body, init, (per_example, given["loss_target"]))
    with _jax.named_scope("update"):
        delta_w, new_m, new_v = {}, {}, {}
        for n in TWIN_WEIGHTS:
            delta_w[n], new_m[n], new_v[n] = _adamw(weights[n], grad_w[n], given["m_" + n], given["v_" + n])
    return (loss, grad_x, *[grad_w[n] for n in TWIN_WEIGHTS], *[delta_w[n] for n in TWIN_WEIGHTS],
            *[new_m[n] for n in TWIN_WEIGHTS], *[new_v[n] for n in TWIN_WEIGHTS])
```

```python
import functools

import jax
import jax.numpy as jnp
from jax import lax
from jax.experimental import pallas as pl
from jax.experimental.pallas import tpu as pltpu

F32 = jnp.float32
BF16 = jnp.bfloat16
_MXU = jnp.bfloat16

D_MODEL = 1024
DEPTH = 4
MEM_LEN = 256
CHUNK = 128
A_GROUPS = 4
A_WIDTH = 512
B_WIDTH = 512
GROUP_DIM = 128
CONV_WIDTH = 31
C_HEADS = 16
C_HEAD_DIM = 64
MEM_HEADS = 4
MEM_HEAD_DIM = 256
D_FF = 2816
ALPHA = (2.0 * DEPTH) ** 0.25
LN_EPS = 1e-5
SB_SCALE = C_HEAD_DIM ** -0.5
MEM_SCALE = MEM_HEAD_DIM ** -0.5

ADAM_LR = 0.001
ADAM_B1 = 0.9
ADAM_B2 = 0.999
ADAM_EPS = 1e-08
ADAM_WD = 0.01
ADAM_STEP = 10

N_DEV = 8
AXES = ("x", "y", "c")
LANE = 128
SUBLANE = 8
PACK_COLS = 1024
_VMEM_LIMIT = 56 * 1024 * 1024

SB_TQ = 256
SB_TK = 128
CONV_ROWS = 256
CONV_PAD = 32


def _tile(n, cap, mult):
    best = None
    for d in range(mult, min(n, cap) + 1, mult):
        if n % d == 0:
            best = d
    return n if best is None else best


def _call(body, *, name, grid, in_specs, out_specs, out_shape, scratch=(), sem=None):
    return pl.pallas_call(
        body, name=name, grid=grid, in_specs=in_specs, out_specs=out_specs, out_shape=out_shape,
        scratch_shapes=list(scratch),
        compiler_params=pltpu.CompilerParams(dimension_semantics=sem, vmem_limit_bytes=_VMEM_LIMIT))


def _dg(a, b, ca, cb):
    return lax.dot_general(a.astype(_MXU), b.astype(_MXU), (((ca,), (cb,)), ((), ())),
                           preferred_element_type=F32)


@jax.custom_vjp
def _dot(a, b):
    return _dg(a, b, 1, 0)


_dot.defvjp(lambda a, b: (_dg(a, b, 1, 0), (a, b)),
            lambda r, g: (_dg(g, r[1], 1, 1), _dg(r[0], g, 0, 0)))


@jax.custom_vjp
def _dot_t(a, b):
    return _dg(a, b, 1, 1)


_dot_t.defvjp(lambda a, b: (_dg(a, b, 1, 1), (a, b)),
              lambda r, g: (_dg(g, r[1], 1, 0), _dg(g, r[0], 0, 0)))


def _norm(z, g, b):
    mu = jnp.mean(z, axis=-1, keepdims=True)
    zc = z - mu
    var = jnp.mean(zc * zc, axis=-1, keepdims=True)
    return zc * lax.rsqrt(var + LN_EPS) * g + b


def _gelu(x):
    return 0.5 * x * (1.0 + lax.erf(x * (0.5 ** 0.5)))


def _matmul(a, b, mode="nn", add=None, out_dtype=F32, la=None, lb=None, name="mm"):
    ash, bsh = a.shape[-2:], b.shape[-2:]
    if mode == "nn":
        (M, K), (K2, N) = ash, bsh
    elif mode == "nt":
        (M, K), (N, K2) = ash, bsh
    else:
        (K, M), (K2, N) = ash, bsh
    assert K == K2, (a.shape, b.shape, mode)
    tm = _tile(M, 512, LANE)
    tn = _tile(N, 1408, LANE)
    tk = _tile(K, 512 if mode == "tn" else 1408, LANE)
    nk = K // tk
    if mode == "tn":
        a_blk, a_idx = (tk, tm), (lambda i, j, k: (k, i))
    else:
        a_blk, a_idx = (tm, tk), (lambda i, j, k: (i, k))
    if mode == "nt":
        b_blk, b_idx = (tn, tk), (lambda i, j, k: (j, k))
    else:
        b_blk, b_idx = (tk, tn), (lambda i, j, k: (k, j))
    dims = {"nn": (1, 0), "nt": (1, 1), "tn": (0, 0)}[mode]

    def spec(blk, idx, lead):
        if lead is None:
            return pl.BlockSpec(blk, idx)
        return pl.BlockSpec((None,) + blk, lambda i, j, k: (lead,) + idx(i, j, k))

    has_add = add is not None

    def body(*refs):
        a_ref, b_ref = refs[0], refs[1]
        add_ref = refs[2] if has_add else None
        o_ref = refs[3] if has_add else refs[2]
        p = _dg(a_ref[...], b_ref[...], *dims)

        def finish(r):
            if has_add:
                r = r + add_ref[...].astype(F32)
            o_ref[...] = r.astype(out_dtype)

        if nk == 1:
            finish(p)
        else:
            acc = refs[-1]
            k = pl.program_id(2)

            @pl.when(k == 0)
            def _():
                acc[...] = p

            @pl.when(k > 0)
            def _():
                acc[...] += p

            @pl.when(k == nk - 1)
            def _():
                finish(acc[...])

    in_specs = [spec(a_blk, a_idx, la), spec(b_blk, b_idx, lb)]
    args = [a, b]
    if has_add:
        in_specs.append(pl.BlockSpec((tm, tn), lambda i, j, k: (i, j)))
        args.append(add)
    return _call(
        body, name=name, grid=(M // tm, N // tn, nk), in_specs=in_specs,
        out_specs=pl.BlockSpec((tm, tn), lambda i, j, k: (i, j)),
        out_shape=jax.ShapeDtypeStruct((M, N), out_dtype),
        scratch=[pltpu.VMEM((tm, tn), F32)] if nk > 1 else [],
        sem=("parallel", "parallel", "arbitrary"))(*args)


def _ln_fn(x, f, g, b):
    return _norm(ALPHA * x + f, g, b)


def _ln_fwd(x, f, g, b):
    T, D = x.shape
    tm = _tile(T, 256, SUBLANE)

    def body(x_ref, f_ref, g_ref, b_ref, o_ref):
        o_ref[...] = _ln_fn(x_ref[...], f_ref[...], g_ref[...], b_ref[...])

    row = pl.BlockSpec((tm, D), lambda i: (i, 0))
    par = pl.BlockSpec((1, D), lambda i: (0, 0))
    return _call(body, name="ln_fwd", grid=(T // tm,), in_specs=[row, row, par, par], out_specs=row,
                 out_shape=jax.ShapeDtypeStruct((T, D), F32), sem=("parallel",))(x, f, g, b)


def _ln_bwd(x, f, g, b, dy):
    T, D = x.shape
    tm = _tile(T, 256, SUBLANE)

    def body(x_ref, f_ref, g_ref, b_ref, dy_ref, dx_ref, df_ref, dg_ref, db_ref):
        _, vjp = jax.vjp(_ln_fn, x_ref[...], f_ref[...], g_ref[...], b_ref[...])
        dx, df, dg, db = vjp(dy_ref[...])
        dx_ref[...] = dx
        df_ref[...] = df.astype(BF16)

        @pl.when(pl.program_id(0) == 0)
        def _():
            dg_ref[...] = jnp.zeros_like(dg_ref)
            db_ref[...] = jnp.zeros_like(db_ref)

        dg_ref[...] += dg
        db_ref[...] += db

    row = pl.BlockSpec((tm, D), lambda i: (i, 0))
    par = pl.BlockSpec((1, D), lambda i: (0, 0))
    return _call(body, name="ln_bwd", grid=(T // tm,), in_specs=[row, row, par, par, row],
                 out_specs=[row, row, par, par],
                 out_shape=[jax.ShapeDtypeStruct((T, D), F32), jax.ShapeDtypeStruct((T, D), BF16),
                            jax.ShapeDtypeStruct((1, D), F32), jax.ShapeDtypeStruct((1, D), F32)],
                 sem=("arbitrary",))(x, f, g, b, dy)


def _loss_fwd(y, tgt):
    T, D = y.shape
    tm = _tile(T, 256, SUBLANE)

    def body(y_ref, t_ref, l_ref, dy_ref):
        d = y_ref[...] - t_ref[...]
        dy_ref[...] = d * (1.0 / D)

        @pl.when(pl.program_id(0) == 0)
        def _():
            l_ref[...] = jnp.zeros_like(l_ref)

        l_ref[...] += jnp.sum(d * d, axis=0, keepdims=True) * (0.5 / D)

    row = pl.BlockSpec((tm, D), lambda i: (i, 0))
    par = pl.BlockSpec((1, D), lambda i: (0, 0))
    return _call(body, name="loss", grid=(T // tm,), in_specs=[row, row], out_specs=[par, row],
                 out_shape=[jax.ShapeDtypeStruct((1, D), F32), jax.ShapeDtypeStruct((T, D), F32)],
                 sem=("arbitrary",))(y, tgt)


def _swiglu_fn(h1, h3):
    return h1 * jax.nn.sigmoid(h1) * h3


def _swiglu_fwd(h13):
    T, F2 = h13.shape
    F = F2 // 2
    tm = _tile(T, 256, SUBLANE)

    def body(h1_ref, h3_ref, o_ref):
        o_ref[...] = _swiglu_fn(h1_ref[...], h3_ref[...]).astype(BF16)

    return _call(body, name="swiglu_fwd", grid=(T // tm,),
                 in_specs=[pl.BlockSpec((tm, F), lambda i: (i, 0)), pl.BlockSpec((tm, F), lambda i: (i, 1))],
                 out_specs=pl.BlockSpec((tm, F), lambda i: (i, 0)),
                 out_shape=jax.ShapeDtypeStruct((T, F), BF16), sem=("parallel",))(h13, h13)


def _swiglu_bwd(h13, dact):
    T, F2 = h13.shape
    F = F2 // 2
    tm = _tile(T, 256, SUBLANE)

    def body(h1_ref, h3_ref, d_ref, o_ref):
        _, vjp = jax.vjp(_swiglu_fn, h1_ref[...], h3_ref[...])
        d1, d3 = vjp(d_ref[...])
        o_ref[:, :F] = d1.astype(BF16)
        o_ref[:, F:] = d3.astype(BF16)

    return _call(body, name="swiglu_bwd", grid=(T // tm,),
                 in_specs=[pl.BlockSpec((tm, F), lambda i: (i, 0)), pl.BlockSpec((tm, F), lambda i: (i, 1)),
                           pl.BlockSpec((tm, F), lambda i: (i, 0))],
                 out_specs=pl.BlockSpec((tm, F2), lambda i: (i, 0)),
                 out_shape=jax.ShapeDtypeStruct((T, F2), BF16), sem=("parallel",))(h13, h13, dact)


@jax.custom_vjp
def _chunkmix(wm, vn, bs):
    n = vn.shape[0] // CHUNK
    return jnp.concatenate([_dg(wm, vn[c * CHUNK:(c + 1) * CHUNK], 1, 0) + bs for c in range(n)], axis=0)


def _chunkmix_fwd(wm, vn, bs):
    return _chunkmix(wm, vn, bs), (wm, vn)


def _chunkmix_bwd(res, ct):
    wm, vn = res
    n = vn.shape[0] // CHUNK
    cts = [ct[c * CHUNK:(c + 1) * CHUNK] for c in range(n)]
    dvn = jnp.concatenate([_dg(wm, cts[c], 0, 0) for c in range(n)], axis=0)
    dwm = sum(_dg(cts[c], vn[c * CHUNK:(c + 1) * CHUNK], 1, 1) for c in range(n))
    dbs = sum(jnp.sum(cts[c], axis=1, keepdims=True) for c in range(n))
    return dwm, dvn, dbs


_chunkmix.defvjp(_chunkmix_fwd, _chunkmix_bwd)


def _gmlp_fn(u, v, lg, lb, ws, bs):
    ug = _gelu(u)
    vn = _norm(_gelu(v), lg, lb)
    r = lax.broadcasted_iota(jnp.int32, (CHUNK, CHUNK), 0)
    c = lax.broadcasted_iota(jnp.int32, (CHUNK, CHUNK), 1)
    wm = jnp.where(r >= c, ws, 0.0)
    return ug * _chunkmix(wm, vn, bs)


def _gmlp_specs(tm, order):
    def at(col0):
        return pl.BlockSpec((tm, GROUP_DIM), lambda *ids: (order(*ids)[0], col0 + order(*ids)[1]))
    par = pl.BlockSpec((1, GROUP_DIM), lambda *ids: (0, order(*ids)[1]))
    ws = pl.BlockSpec((None, CHUNK, CHUNK), lambda *ids: (order(*ids)[1], 0, 0))
    bs = pl.BlockSpec((None, CHUNK, 1), lambda *ids: (order(*ids)[1], 0, 0))
    return at, par, ws, bs


def _gmlp_fwd(hin, lg, lb, ws, bs):
    T = hin.shape[0]
    tm = _tile(T, 512, CHUNK)
    at, par, wsp, bsp = _gmlp_specs(tm, lambda i, g: (i, g))

    def body(u_ref, v_ref, lg_ref, lb_ref, ws_ref, bs_ref, o_ref):
        o_ref[...] = _gmlp_fn(u_ref[...], v_ref[...], lg_ref[...], lb_ref[...], ws_ref[...],
                              bs_ref[...]).astype(BF16)

    return _call(body, name="gmlp_fwd", grid=(T // tm, A_GROUPS),
                 in_specs=[at(0), at(A_GROUPS), par, par, wsp, bsp], out_specs=at(0),
                 out_shape=jax.ShapeDtypeStruct((T, A_WIDTH), BF16),
                 sem=("parallel", "parallel"))(hin, hin, lg, lb, ws, bs)


def _gmlp_bwd(hin, lg, lb, ws, bs, dyab):
    T = hin.shape[0]
    tm = _tile(T, 512, CHUNK)
    at, par, wsp, bsp = _gmlp_specs(tm, lambda g, i: (i, g))

    def body(u_ref, v_ref, lg_ref, lb_ref, ws_ref, bs_ref, dy_ref,
             du_ref, dv_ref, dlg_ref, dlb_ref, dws_ref, dbs_ref):
        _, vjp = jax.vjp(_gmlp_fn, u_ref[...], v_ref[...], lg_ref[...], lb_ref[...], ws_ref[...], bs_ref[...])
        du, dv, dlg, dlb, dws, dbs = vjp(dy_ref[...])
        du_ref[...] = du.astype(BF16)
        dv_ref[...] = dv.astype(BF16)

        @pl.when(pl.program_id(1) == 0)
        def _():
            dlg_ref[...] = jnp.zeros_like(dlg_ref)
            dlb_ref[...] = jnp.zeros_like(dlb_ref)
            dws_ref[...] = jnp.zeros_like(dws_ref)
            dbs_ref[...] = jnp.zeros_like(dbs_ref)

        dlg_ref[...] += dlg
        dlb_ref[...] += dlb
        dws_ref[...] += dws
        dbs_ref[...] += dbs

    half = jax.ShapeDtypeStruct((T, A_WIDTH), BF16)
    return _call(body, name="gmlp_bwd", grid=(A_GROUPS, T // tm),
                 in_specs=[at(0), at(A_GROUPS), par, par, wsp, bsp, at(0)],
                 out_specs=[at(0), at(0), par, par, wsp, bsp],
                 out_shape=[half, half, jax.ShapeDtypeStruct((1, A_WIDTH), F32),
                            jax.ShapeDtypeStruct((1, A_WIDTH), F32),
                            jax.ShapeDtypeStruct((A_GROUPS, CHUNK, CHUNK), F32),
                            jax.ShapeDtypeStruct((A_GROUPS, CHUNK, 1), F32)],
                 sem=("parallel", "arbitrary"))(hin, hin, lg, lb, ws, bs, dyab)


def _glu_fn(a, gt):
    return a * jax.nn.sigmoid(gt)


def _gn_silu_fn(c, cb, gg, gb):
    y = _norm(c + cb, gg, gb)
    return y * jax.nn.sigmoid(y)


def _conv_taps(w_ref, src_ref, row0, first):
    acc = None
    for k in range(CONV_WIDTH):
        term = w_ref[pl.ds(k, 1), :] * src_ref[pl.ds(row0 + first(k), CONV_ROWS), :]
        acc = term if acc is None else acc + term
    return acc


def _conv_specs(S, order):
    def at(col0):
        return pl.BlockSpec((S, GROUP_DIM), lambda *ids: (order(*ids)[0], col0 + order(*ids)[1]))
    par = pl.BlockSpec((1, GROUP_DIM), lambda *ids: (0, order(*ids)[1]))
    cw = pl.BlockSpec((CONV_WIDTH, GROUP_DIM), lambda *ids: (0, order(*ids)[1]))
    return at, par, cw


def _conv_fwd(hin, cw, cb, gg, gb, n_ex, S):
    T = hin.shape[0]
    at, par, cwp = _conv_specs(S, lambda e, g: (e, g))
    ngrp = B_WIDTH // GROUP_DIM
    lead = CONV_PAD - (CONV_WIDTH - 1)

    def body(a_ref, gt_ref, cw_ref, cb_ref, gg_ref, gb_ref, o_ref, pad_ref):
        pad_ref[0:CONV_PAD, :] = jnp.zeros((CONV_PAD, GROUP_DIM), F32)
        for r in range(S // CONV_ROWS):
            rows = pl.ds(r * CONV_ROWS, CONV_ROWS)
            pad_ref[pl.ds(CONV_PAD + r * CONV_ROWS, CONV_ROWS), :] = _glu_fn(a_ref[rows, :], gt_ref[rows, :])
        for r in range(S // CONV_ROWS):
            c = _conv_taps(cw_ref, pad_ref, r * CONV_ROWS, lambda k: lead + k)
            o_ref[pl.ds(r * CONV_ROWS, CONV_ROWS), :] = _gn_silu_fn(
                c, cb_ref[...], gg_ref[...], gb_ref[...]).astype(BF16)

    return _call(body, name="conv_fwd", grid=(n_ex, ngrp),
                 in_specs=[at(2 * A_GROUPS), at(2 * A_GROUPS + ngrp), cwp, par, par, par], out_specs=at(0),
                 out_shape=jax.ShapeDtypeStruct((T, B_WIDTH), BF16),
                 scratch=[pltpu.VMEM((S + CONV_PAD, GROUP_DIM), F32)],
                 sem=("parallel", "parallel"))(hin, hin, cw, cb, gg, gb)


def _conv_bwd(hin, cw, cb, gg, gb, dyab, n_ex, S):
    T = hin.shape[0]
    at, par, cwp = _conv_specs(S, lambda g, e: (e, g))
    ngrp = B_WIDTH // GROUP_DIM
    lead = CONV_PAD - (CONV_WIDTH - 1)
    nchunk = S // CONV_ROWS

    def body(a_ref, gt_ref, cw_ref, cb_ref, gg_ref, gb_ref, dy_ref,
             da_ref, dgt_ref, dcw_ref, dcb_ref, dgg_ref, dgb_ref, pad_ref, dcp_ref, wacc_ref, pacc_ref):
        @pl.when(pl.program_id(1) == 0)
        def _():
            wacc_ref[...] = jnp.zeros_like(wacc_ref)
            pacc_ref[...] = jnp.zeros_like(pacc_ref)

        pad_ref[0:CONV_PAD, :] = jnp.zeros((CONV_PAD, GROUP_DIM), F32)
        dcp_ref[S:S + CONV_PAD, :] = jnp.zeros((CONV_PAD, GROUP_DIM), F32)
        for r in range(nchunk):
            rows = pl.ds(r * CONV_ROWS, CONV_ROWS)
            pad_ref[pl.ds(CONV_PAD + r * CONV_ROWS, CONV_ROWS), :] = _glu_fn(a_ref[rows, :], gt_ref[rows, :])
        for r in range(nchunk):
            rows = pl.ds(r * CONV_ROWS, CONV_ROWS)
            c = _conv_taps(cw_ref, pad_ref, r * CONV_ROWS, lambda k: lead + k)
            _, vjp = jax.vjp(_gn_silu_fn, c, cb_ref[...], gg_ref[...], gb_ref[...])
            dc, dcb, dgg, dgb = vjp(dy_ref[rows, :])
            dcp_ref[rows, :] = dc
            pacc_ref[0:1, :] += dcb
            pacc_ref[1:2, :] += dgg
            pacc_ref[2:3, :] += dgb
        for r in range(nchunk):
            rows = pl.ds(r * CONV_ROWS, CONV_ROWS)
            dh = _conv_taps(cw_ref, dcp_ref, r * CONV_ROWS, lambda k: CONV_WIDTH - 1 - k)
            _, vjp = jax.vjp(_glu_fn, a_ref[rows, :], gt_ref[rows, :])
            da, dgt = vjp(dh)
            da_ref[rows, :] = da.astype(BF16)
            dgt_ref[rows, :] = dgt.astype(BF16)
            dc = dcp_ref[rows, :]
            for k in range(CONV_WIDTH):
                prod = dc * pad_ref[pl.ds(r * CONV_ROWS + lead + k, CONV_ROWS), :]
                wacc_ref[k] += jnp.sum(prod.reshape(CONV_ROWS // SUBLANE, SUBLANE, GROUP_DIM), axis=0)
        for k in range(CONV_WIDTH):
            dcw_ref[pl.ds(k, 1), :] = jnp.sum(wacc_ref[k], axis=0, keepdims=True)
        dcb_ref[...] = pacc_ref[0:1, :]
        dgg_ref[...] = pacc_ref[1:2, :]
        dgb_ref[...] = pacc_ref[2:3, :]

    half = jax.ShapeDtypeStruct((T, B_WIDTH), BF16)
    vec = jax.ShapeDtypeStruct((1, B_WIDTH), F32)
    return _call(body, name="conv_bwd", grid=(ngrp, n_ex),
                 in_specs=[at(2 * A_GROUPS), at(2 * A_GROUPS + ngrp), cwp, par, par, par, at(ngrp)],
                 out_specs=[at(0), at(0), cwp, par, par, par],
                 out_shape=[half, half, jax.ShapeDtypeStruct((CONV_WIDTH, B_WIDTH), F32), vec, vec, vec],
                 scratch=[pltpu.VMEM((S + CONV_PAD, GROUP_DIM), F32), pltpu.VMEM((S + CONV_PAD, GROUP_DIM), F32),
                          pltpu.VMEM((CONV_WIDTH, SUBLANE, GROUP_DIM), F32), pltpu.VMEM((SUBLANE, GROUP_DIM), F32)],
                 sem=("parallel", "arbitrary"))(hin, hin, cw, cb, gg, gb, dyab)


def _sb_consts():
    r = lax.broadcasted_iota(jnp.int32, (SB_TK + SUBLANE, 2 * SB_TK), 0)
    c = lax.broadcasted_iota(jnp.int32, (SB_TK + SUBLANE, 2 * SB_TK), 1) % SB_TK
    tail = r >= SB_TK
    u_after = jnp.where((c > r) | tail, 1.0, 0.0).astype(BF16)
    u_before = jnp.where((c < r) | tail, 1.0, 0.0).astype(BF16)
    s = lax.broadcasted_iota(jnp.int32, (SB_TK, SB_TQ), 0)
    t = lax.broadcasted_iota(jnp.int32, (SB_TK, SB_TQ), 1)
    masks = [(s + SB_TK * d) < t for d in range(SB_TQ // SB_TK)]
    return u_after, u_before, masks


def _split_sum(u, x):
    hi = x.astype(BF16)
    lo = (x - hi.astype(F32)).astype(BF16)
    res = _dg(u, jnp.concatenate([hi, lo], axis=0), 1, 0)
    return res[:SB_TK], res[SB_TK:]


def _add8(x, row8):
    return (x.reshape(-1, SUBLANE, x.shape[-1]) + row8[None]).reshape(x.shape)


def _sb_scores(kj, qi, mask):
    z = _dg(kj, qi, 0, 0) * SB_SCALE
    e = jnp.exp(-jnp.abs(z))
    sp = jnp.maximum(z, 0.0) + jnp.log1p(e)
    lognot = -sp if mask is None else jnp.where(mask, -sp, 0.0)
    return z, e, sp, lognot


def _sb_fwd(qkvT, n_ex, S):
    T = qkvT.shape[-1]
    nq, nk = S // SB_TQ, S // SB_TK
    per = SB_TQ // SB_TK

    def body(q_ref, k_ref, v_ref, o_ref, qb, kb, vb, ob):
        for i in range(nq):
            qb[i] = q_ref[:, i * SB_TQ:(i + 1) * SB_TQ].astype(_MXU)
        for j in range(nk):
            kb[j] = k_ref[:, j * SB_TK:(j + 1) * SB_TK].astype(_MXU)
            vb[j] = v_ref[:, j * SB_TK:(j + 1) * SB_TK].astype(_MXU)
        u_after, _, masks = _sb_consts()

        def tile(j, qi, carry, acc, mask):
            z, _, sp, lognot = _sb_scores(kb[j], qi, mask)
            within, total = _split_sum(u_after, lognot)
            att = jnp.exp((z - sp) + _add8(within, carry))
            if mask is not None:
                att = jnp.where(mask, att, 0.0)
            return carry + total, acc + _dg(vb[j], att, 1, 0)

        def qtile(i, _):
            qi = qb[i]
            carry = jnp.zeros((SUBLANE, SB_TQ), F32)
            acc = jnp.zeros((C_HEAD_DIM, SB_TQ), F32)
            for d in reversed(range(per)):
                carry, acc = tile(per * i + d, qi, carry, acc, masks[d])
            carry, acc = lax.fori_loop(0, per * i, lambda jj, st: tile(per * i - 1 - jj, qi, st[0], st[1], None),
                                       (carry, acc))
            ob[i] = acc
            return 0

        lax.fori_loop(0, nq, qtile, 0)
        for i in range(nq):
            o_ref[:, i * SB_TQ:(i + 1) * SB_TQ] = ob[i]

    def at(part):
        return pl.BlockSpec((None, C_HEAD_DIM, S), lambda e, h: (part * C_HEADS + h, 0, e))

    return _call(body, name="sb_fwd", grid=(n_ex, C_HEADS), in_specs=[at(0), at(1), at(2)], out_specs=at(0),
                 out_shape=jax.ShapeDtypeStruct((C_HEADS, C_HEAD_DIM, T), F32),
                 scratch=[pltpu.VMEM((nq, C_HEAD_DIM, SB_TQ), _MXU), pltpu.VMEM((nk, C_HEAD_DIM, SB_TK), _MXU),
                          pltpu.VMEM((nk, C_HEAD_DIM, SB_TK), _MXU), pltpu.VMEM((nq, C_HEAD_DIM, SB_TQ), F32)],
                 sem=("parallel", "parallel"))(qkvT, qkvT, qkvT)


def _sb_bwd(qkvT, doT, n_ex, S):
    T = qkvT.shape[-1]
    nq, nk = S // SB_TQ, S // SB_TK
    per = SB_TQ // SB_TK

    def body(q_ref, k_ref, v_ref, do_ref, dq_ref, dk_ref, dv_ref, qb, kb, vb, dob, dqa, dka, dva, dl_s, sg_s):
        for i in range(nq):
            cols = slice(i * SB_TQ, (i + 1) * SB_TQ)
            qb[i] = q_ref[:, cols].astype(_MXU)
            dob[i] = do_ref[:, cols].astype(_MXU)
        for j in range(nk):
            kb[j] = k_ref[:, j * SB_TK:(j + 1) * SB_TK].astype(_MXU)
            vb[j] = v_ref[:, j * SB_TK:(j + 1) * SB_TK].astype(_MXU)
        dka[...] = jnp.zeros_like(dka)
        dva[...] = jnp.zeros_like(dva)
        u_after, u_before, masks = _sb_consts()

        def rebuild(j, qi, gi, carry, mask):
            z, e, sp, lognot = _sb_scores(kb[j], qi, mask)
            within, total = _split_sum(u_after, lognot)
            att = jnp.exp((z - sp) + _add8(within, carry))
            r = 1.0 / (1.0 + e)
            sig = jnp.where(z >= 0.0, r, e * r)
            if mask is not None:
                att = jnp.where(mask, att, 0.0)
                sig = jnp.where(mask, sig, 0.0)
            dl_s[j] = _dg(vb[j], gi, 0, 0) * att
            sg_s[j] = sig
            dva[j] += _dg(gi, att, 1, 1)
            return carry + total

        def push(j, qi, carry, dq):
            dlogit, sig = dl_s[j], sg_s[j]
            within, total = _split_sum(u_before, dlogit)
            dz = (dlogit * (1.0 - sig) - sig * _add8(within, carry)) * SB_SCALE
            dka[j] += _dg(qi, dz, 1, 1)
            return carry + total, dq + _dg(kb[j], dz, 1, 0)

        def qtile(i, _):
            qi, gi = qb[i], dob[i]
            carry = jnp.zeros((SUBLANE, SB_TQ), F32)
            for d in reversed(range(per)):
                carry = rebuild(per * i + d, qi, gi, carry, masks[d])
            lax.fori_loop(0, per * i, lambda jj, c: rebuild(per * i - 1 - jj, qi, gi, c, None), carry)
            st = (jnp.zeros((SUBLANE, SB_TQ), F32), jnp.zeros((C_HEAD_DIM, SB_TQ), F32))
            st = lax.fori_loop(0, per * (i + 1), lambda j, s: push(j, qi, s[0], s[1]), st)
            dqa[i] = st[1]
            return 0

        lax.fori_loop(0, nq, qtile, 0)
        for i in range(nq):
            dq_ref[:, i * SB_TQ:(i + 1) * SB_TQ] = dqa[i].astype(BF16)
        for j in range(nk):
            dk_ref[:, j * SB_TK:(j + 1) * SB_TK] = dka[j].astype(BF16)
            dv_ref[:, j * SB_TK:(j + 1) * SB_TK] = dva[j].astype(BF16)

    def at(part):
        return pl.BlockSpec((None, C_HEAD_DIM, S), lambda e, h: (part * C_HEADS + h, 0, e))

    out = jax.ShapeDtypeStruct((C_HEADS, C_HEAD_DIM, T), BF16)
    return _call(body, name="sb_bwd", grid=(n_ex, C_HEADS), in_specs=[at(0), at(1), at(2), at(0)],
                 out_specs=[at(0), at(0), at(0)], out_shape=[out, out, out],
                 scratch=[pltpu.VMEM((nq, C_HEAD_DIM, SB_TQ), _MXU), pltpu.VMEM((nk, C_HEAD_DIM, SB_TK), _MXU),
                          pltpu.VMEM((nk, C_HEAD_DIM, SB_TK), _MXU), pltpu.VMEM((nq, C_HEAD_DIM, SB_TQ), _MXU),
                          pltpu.VMEM((nq, C_HEAD_DIM, SB_TQ), F32),
                          pltpu.VMEM((nk, C_HEAD_DIM, SB_TK), F32), pltpu.VMEM((nk, C_HEAD_DIM, SB_TK), F32),
                          pltpu.VMEM((nk, SB_TK, SB_TQ), F32), pltpu.VMEM((nk, SB_TK, SB_TQ), F32)],
                 sem=("parallel", "parallel"))(qkvT, qkvT, qkvT, doT)


def _xattn_fn(q, k, v):
    s = _dot_t(q, k) * MEM_SCALE
    m = lax.stop_gradient(jnp.max(s, axis=-1, keepdims=True))
    p = jnp.exp(s - m)
    p = p / jnp.sum(p, axis=-1, keepdims=True)
    return _dot(p, v)


def _xattn_fwd(q, kk, vv, n_ex, S):
    T = q.shape[0]
    tq = _tile(S, 1024, SUBLANE)
    nqt = S // tq

    def body(q_ref, k_ref, v_ref, o_ref):
        o_ref[...] = _xattn_fn(q_ref[...], k_ref[...], v_ref[...]).astype(BF16)

    qs = pl.BlockSpec((tq, MEM_HEAD_DIM), lambda e, i, h: (e * nqt + i, h))
    ks = pl.BlockSpec((MEM_LEN, MEM_HEAD_DIM), lambda e, i, h: (e, h))
    return _call(body, name="xattn_fwd", grid=(n_ex, nqt, MEM_HEADS), in_specs=[qs, ks, ks], out_specs=qs,
                 out_shape=jax.ShapeDtypeStruct((T, D_MODEL), BF16),
                 sem=("parallel", "parallel", "parallel"))(q, kk, vv)


def _xattn_bwd(q, kk, vv, do, n_ex, S):
    T = q.shape[0]
    tq = _tile(S, 1024, SUBLANE)
    nqt = S // tq

    def body(q_ref, k_ref, v_ref, do_ref, dq_ref, dk_ref, dv_ref):
        _, vjp = jax.vjp(_xattn_fn, q_ref[...], k_ref[...], v_ref[...])
        dq, dk, dv = vjp(do_ref[...])
        dq_ref[...] = dq.astype(BF16)

        @pl.when(pl.program_id(2) == 0)
        def _():
            dk_ref[...] = jnp.zeros_like(dk_ref)
            dv_ref[...] = jnp.zeros_like(dv_ref)

        dk_ref[...] += dk
        dv_ref[...] += dv

    qs = pl.BlockSpec((tq, MEM_HEAD_DIM), lambda e, h, i: (e * nqt + i, h))
    ks = pl.BlockSpec((MEM_LEN, MEM_HEAD_DIM), lambda e, h, i: (e, h))
    kv = jax.ShapeDtypeStruct(kk.shape, F32)
    return _call(body, name="xattn_bwd", grid=(n_ex, MEM_HEADS, nqt), in_specs=[qs, ks, ks, qs],
                 out_specs=[qs, ks, ks], out_shape=[jax.ShapeDtypeStruct((T, D_MODEL), BF16), kv, kv],
                 sem=("parallel", "parallel", "arbitrary"))(q, kk, vv, do)


def _local_step(x, mem, tgt, W):
    n_ex, S, D = x.shape
    T = n_ex * S
    h = x.reshape(T, D)
    mem2 = mem.reshape(n_ex * MEM_LEN, D)
    row = lambda v: v.reshape(1, -1)
    saved = []
    for l in range(DEPTH):
        sv = dict(x0=h)
        if l % 2 == 0:
            e = l // 2
            hin = _matmul(h, W["w_in_ab"], lb=e, name="in_proj")
            ya = _gmlp_fwd(hin, row(W["gmlp_ln_g"][e]), row(W["gmlp_ln_b"][e]), W["gmlp_w_s"][e],
                           W["gmlp_b_s"][e][:, :, None])
            yb = _conv_fwd(hin, W["conv_w"][e], row(W["conv_b"][e]), row(W["conv_gn_g"][e]),
                           row(W["conv_gn_b"][e]), n_ex, S)
            yab = jnp.concatenate([ya, yb], axis=1)
            mix = _matmul(yab, W["w_out_ab"], lb=e, name="out_proj")
            sv.update(hin=hin, yab=yab)
        else:
            o = l // 2
            qkvT = _matmul(W["w_qkv_cT"], h, mode="nt", la=o, name="qkv_proj")
            yT = _sb_fwd(qkvT.reshape(3 * C_HEADS, C_HEAD_DIM, T), n_ex, S)
            mix = _matmul(yT.reshape(D, T), W["w_out_c"], mode="tn", lb=o, name="sb_out_proj")
            sv.update(qkvT=qkvT, yT=yT)
        x1 = _ln_fwd(h, mix, row(W["ln_g"][l, 0]), row(W["ln_b"][l, 0]))
        q = _matmul(x1, W["mem_wq"], lb=l, name="mem_q")
        kk = _matmul(mem2, W["mem_wk"], lb=l, name="mem_kv")
        vv = _matmul(mem2, W["mem_wv"], lb=l, name="mem_kv")
        oc = _xattn_fwd(q, kk, vv, n_ex, S)
        cross = _matmul(oc, W["mem_wo"], lb=l, name="mem_o")
        x2 = _ln_fwd(x1, cross, row(W["ln_g"][l, 1]), row(W["ln_b"][l, 1]))
        h13 = _matmul(x2, W["ffn_w13"], lb=l, name="ffn_in")
        act = _swiglu_fwd(h13)
        f = _matmul(act, W["ffn_w2"], lb=l, name="ffn_out")
        x3 = _ln_fwd(x2, f, row(W["ln_g"][l, 2]), row(W["ln_b"][l, 2]))
        sv.update(mix=mix, x1=x1, q=q, kk=kk, vv=vv, oc=oc, cross=cross, x2=x2, h13=h13, act=act, f=f)
        saved.append(sv)
        h = x3

    loss_part, dh = _loss_fwd(h, tgt.reshape(T, D))

    G = {k: [None] * n for k, n in [
        ("w_in_ab", 2), ("gmlp_ln_g", 2), ("gmlp_ln_b", 2), ("gmlp_w_s", 2), ("gmlp_b_s", 2), ("conv_w", 2),
        ("conv_b", 2), ("conv_gn_g", 2), ("conv_gn_b", 2), ("w_out_ab", 2), ("w_qkv_c", 2), ("w_out_c", 2),
        ("mem_wq", 4), ("mem_wk", 4), ("mem_wv", 4), ("mem_wo", 4), ("ffn_w1", 4), ("ffn_w3", 4),
        ("ffn_w2", 4), ("ln_g", 4), ("ln_b", 4)]}
    for l in reversed(range(DEPTH)):
        sv = saved[l]
        lng, lnb = [None] * 3, [None] * 3
        dx2, df, lng[2], lnb[2] = _ln_bwd(sv["x2"], sv["f"], row(W["ln_g"][l, 2]), row(W["ln_b"][l, 2]), dh)
        dact = _matmul(df, W["ffn_w2"], mode="nt", lb=l, name="ffn_out_dx")
        G["ffn_w2"][l] = _matmul(sv["act"], df, mode="tn", name="ffn_out_dw")
        dh13 = _swiglu_bwd(sv["h13"], dact)
        dx2 = _matmul(dh13, W["ffn_w13"], mode="nt", lb=l, add=dx2, name="ffn_in_dx")
        dw13 = _matmul(sv["x2"], dh13, mode="tn", name="ffn_in_dw")
        G["ffn_w1"][l], G["ffn_w3"][l] = dw13[:, :D_FF], dw13[:, D_FF:]
        dx1, dcross, lng[1], lnb[1] = _ln_bwd(sv["x1"], sv["cross"], row(W["ln_g"][l, 1]),
                                              row(W["ln_b"][l, 1]), dx2)
        doc = _matmul(dcross, W["mem_wo"], mode="nt", lb=l, name="mem_o_dx")
        G["mem_wo"][l] = _matmul(sv["oc"], dcross, mode="tn", name="mem_o_dw")
        dq, dkk, dvv = _xattn_bwd(sv["q"], sv["kk"], sv["vv"], doc, n_ex, S)
        dx1 = _matmul(dq, W["mem_wq"], mode="nt", lb=l, add=dx1, name="mem_q_dx")
        G["mem_wq"][l] = _matmul(sv["x1"], dq, mode="tn", name="mem_q_dw")
        G["mem_wk"][l] = _matmul(mem2, dkk, mode="tn", name="mem_kv_dw")
        G["mem_wv"][l] = _matmul(mem2, dvv, mode="tn", name="mem_kv_dw")
        dx0, dmix, lng[0], lnb[0] = _ln_bwd(sv["x0"], sv["mix"], row(W["ln_g"][l, 0]), row(W["ln_b"][l, 0]), dx1)
        if l % 2 == 0:
            e = l // 2
            dyab = _matmul(dmix, W["w_out_ab"], mode="nt", lb=e, name="out_proj_dx")
            G["w_out_ab"][e] = _matmul(sv["yab"], dmix, mode="tn", name="out_proj_dw")
            du, dv, dlg, dlb, dws, dbs = _gmlp_bwd(
                sv["hin"], row(W["gmlp_ln_g"][e]), row(W["gmlp_ln_b"][e]), W["gmlp_w_s"][e],
                W["gmlp_b_s"][e][:, :, None], dyab)
            da, dgt, dcw, dcb, dgg, dgb = _conv_bwd(
                sv["hin"], W["conv_w"][e], row(W["conv_b"][e]), row(W["conv_gn_g"][e]),
                row(W["conv_gn_b"][e]), dyab, n_ex, S)
            dhin = jnp.concatenate([du, dv, da, dgt], axis=1)
            dh = _matmul(dhin, W["w_in_ab"], mode="nt", lb=e, add=dx0, name="in_proj_dx")
            G["w_in_ab"][e] = _matmul(sv["x0"], dhin, mode="tn", name="in_proj_dw")
            G["gmlp_ln_g"][e], G["gmlp_ln_b"][e] = dlg[0], dlb[0]
            G["gmlp_w_s"][e], G["gmlp_b_s"][e] = dws, dbs[:, :, 0]
            G["conv_w"][e], G["conv_b"][e], G["conv_gn_g"][e], G["conv_gn_b"][e] = dcw, dcb[0], dgg[0], dgb[0]
        else:
            o = l // 2
            dyT = _matmul(W["w_out_c"], dmix, mode="nt", la=o, name="sb_out_proj_dx")
            G["w_out_c"][o] = _matmul(sv["yT"].reshape(D, T), dmix, name="sb_out_proj_dw")
            dqT, dkT, dvT = _sb_bwd(sv["qkvT"].reshape(3 * C_HEADS, C_HEAD_DIM, T),
                                    dyT.reshape(C_HEADS, C_HEAD_DIM, T), n_ex, S)
            dqkvT = jnp.concatenate([dqT, dkT, dvT], axis=0).reshape(3 * D, T)
            dh = _matmul(dqkvT, W["w_qkv_cT"], mode="tn", lb=o, add=dx0, name="qkv_proj_dx")
            G["w_qkv_c"][o] = _matmul(dqkvT, sv["x0"], name="qkv_proj_dw").T
        G["ln_g"][l] = jnp.concatenate(lng, axis=0)
        G["ln_b"][l] = jnp.concatenate(lnb, axis=0)
    grads = {k: jnp.stack(v, axis=0) for k, v in G.items()}
    return loss_part, dh.reshape(n_ex, S, D), grads


def _exchange(p, bcast, name):
    shape = p.shape[-2:]

    def body(p_ref, out_ref, send_sems, recv_sems, local_sem):
        pos = [lax.axis_index(a) for a in AXES]
        me = 4 * pos[0] + 2 * pos[1] + pos[2]

        def src(idx):
            return p_ref if bcast else p_ref.at[idx]

        local = pltpu.make_async_copy(src(me), out_ref.at[me], local_sem)
        local.start()
        sends, recvs = [], []
        for k in range(1, N_DEV):
            bits = ((k >> 2) & 1, (k >> 1) & 1, k & 1)
            peer = tuple(1 - c if b else c for c, b in zip(pos, bits))
            peer_idx = 4 * peer[0] + 2 * peer[1] + peer[2]
            common = dict(send_sem=send_sems.at[k - 1], recv_sem=recv_sems.at[k - 1], device_id=peer,
                          device_id_type=pl.DeviceIdType.MESH)
            sends.append(pltpu.make_async_remote_copy(src_ref=src(peer_idx), dst_ref=out_ref.at[me], **common))
            recvs.append(pltpu.make_async_remote_copy(src_ref=src(peer_idx), dst_ref=out_ref.at[peer_idx], **common))
        for cp in sends:
            cp.start()
        for cp in recvs:
            cp.wait_recv()
        for cp in sends:
            cp.wait_send()
        local.wait()

    anyspace = pl.BlockSpec(memory_space=pl.ANY)
    return pl.pallas_call(
        body, name=name, in_specs=[anyspace], out_specs=anyspace,
        out_shape=jax.ShapeDtypeStruct((N_DEV,) + shape, p.dtype),
        scratch_shapes=[pltpu.SemaphoreType.DMA((N_DEV - 1,)), pltpu.SemaphoreType.DMA((N_DEV - 1,)),
                        pltpu.SemaphoreType.DMA],
    )(p)


def _adamw(parts, w, m, v):
    R, C = w.shape
    tr = _tile(R, 128, SUBLANE)
    c1 = 1.0 - ADAM_B1 ** ADAM_STEP
    c2 = 1.0 - ADAM_B2 ** ADAM_STEP

    def body(p_ref, w_ref, m_ref, v_ref, g_ref, d_ref, mo_ref, vo_ref):
        g = p_ref[0]
        for s in range(1, N_DEV):
            g = g + p_ref[s]
        mn = ADAM_B1 * m_ref[...] + (1.0 - ADAM_B1) * g
        vn = ADAM_B2 * v_ref[...] + (1.0 - ADAM_B2) * (g * g)
        m_hat = mn / c1
        v_hat = vn / c2
        g_ref[...] = g
        d_ref[...] = -ADAM_LR * (m_hat / (jnp.sqrt(v_hat) + ADAM_EPS) + ADAM_WD * w_ref[...])
        mo_ref[...] = mn
        vo_ref[...] = vn

    row = pl.BlockSpec((tr, C), lambda i: (i, 0))
    out = jax.ShapeDtypeStruct((R, C), F32)
    return _call(body, name="adamw", grid=(R // tr,),
                 in_specs=[pl.BlockSpec((N_DEV, tr, C), lambda i: (0, i, 0)), row, row, row],
                 out_specs=[row, row, row, row], out_shape=[out, out, out, out], sem=("parallel",))(parts, w, m, v)


_MATMUL_W = [("w_in_ab", 2), ("w_out_ab", 1), ("w_qkv_c", 2), ("w_out_c", 1), ("mem_wq", 1), ("mem_wk", 1),
             ("mem_wv", 1), ("mem_wo", 1), ("ffn_w1", 2), ("ffn_w3", 2), ("ffn_w2", 1)]
_SMALL_SHARDED = [("conv_w", 2), ("ln_g", 2), ("ln_b", 2)]
_REPLICATED = ["gmlp_ln_g", "gmlp_ln_b", "gmlp_w_s", "gmlp_b_s", "conv_b", "conv_gn_g", "conv_gn_b"]
_WEIGHTS = ["w_in_ab", "gmlp_ln_g", "gmlp_ln_b", "gmlp_w_s", "gmlp_b_s", "conv_w", "conv_b", "conv_gn_g",
            "conv_gn_b", "w_out_ab", "w_qkv_c", "w_out_c", "mem_wq", "mem_wk", "mem_wv", "mem_wo", "ffn_w1",
            "ffn_w3", "ffn_w2", "ln_g", "ln_b"]
_PACK_ORDER = [n for n, _ in _MATMUL_W] + [n for n, _ in _SMALL_SHARDED] + _REPLICATED
_PACK_ROW_MULT = 128


def _unshard(g, axis):
    full = jnp.moveaxis(g, 0, axis)
    shp = full.shape
    return full.reshape(shp[:axis] + (shp[axis] * shp[axis + 1],) + shp[axis + 2:])


def _split_shards(full, axis):
    shp = full.shape
    parts = full.reshape(shp[:axis] + (N_DEV, shp[axis] // N_DEV) + shp[axis + 1:])
    return jnp.moveaxis(parts, axis, 0)


def _pack_rows(flat_parts, lead=()):
    flat = jnp.concatenate(flat_parts, axis=-1)
    n = flat.shape[-1]
    per = PACK_COLS * _PACK_ROW_MULT
    total = -(-n // per) * per
    if total != n:
        flat = jnp.concatenate([flat, jnp.zeros(lead + (total - n,), flat.dtype)], axis=-1)
    return flat.reshape(lead + (total // PACK_COLS, PACK_COLS))


def _gather_weights(w):
    big = _pack_rows([w[n].astype(BF16).reshape(-1) for n, _ in _MATMUL_W])
    small = _pack_rows([w[n].reshape(-1) for n, _ in _SMALL_SHARDED])
    big_all = _exchange(big, True, "gather_matmul_weights").reshape(N_DEV, -1)
    small_all = _exchange(small, True, "gather_small_weights").reshape(N_DEV, -1)
    full = {n: w[n] for n in _REPLICATED}
    for table, src in ((_MATMUL_W, big_all), (_SMALL_SHARDED, small_all)):
        off = 0
        for n, axis in table:
            size = w[n].size
            full[n] = _unshard(src[:, off:off + size].reshape((N_DEV,) + w[n].shape), axis)
            off += size
    full["w_qkv_cT"] = jnp.swapaxes(full.pop("w_qkv_c"), 1, 2)
    full["ffn_w13"] = jnp.concatenate([full.pop("ffn_w1"), full.pop("ffn_w3")], axis=2)
    return full


def _pack_grads(grads):
    pieces = [_split_shards(grads[n], axis).reshape(N_DEV, -1) for n, axis in _MATMUL_W + _SMALL_SHARDED]
    pieces += [jnp.broadcast_to(grads[n].reshape(1, -1), (N_DEV, grads[n].size)) for n in _REPLICATED]
    return _pack_rows(pieces, lead=(N_DEV,))


def _pack_local(w):
    return _pack_rows([w[n].reshape(-1) for n in _PACK_ORDER])


def _unpack_local(packed, like):
    flat = packed.reshape(-1)
    out, off = {}, 0
    for n in _PACK_ORDER:
        out[n] = flat[off:off + like[n].size].reshape(like[n].shape)
        off += like[n].size
    return out


def kernel(x, mem, w_in_ab, gmlp_ln_g, gmlp_ln_b, gmlp_w_s, gmlp_b_s, conv_w, conv_b, conv_gn_g, conv_gn_b, w_out_ab, w_qkv_c, w_out_c, mem_wq, mem_wk, mem_wv, mem_wo, ffn_w1, ffn_w3, ffn_w2, ln_g, ln_b, loss_target, m_w_in_ab, m_gmlp_ln_g, m_gmlp_ln_b, m_gmlp_w_s, m_gmlp_b_s, m_conv_w, m_conv_b, m_conv_gn_g, m_conv_gn_b, m_w_out_ab, m_w_qkv_c, m_w_out_c, m_mem_wq, m_mem_wk, m_mem_wv, m_mem_wo, m_ffn_w1, m_ffn_w3, m_ffn_w2, m_ln_g, m_ln_b, v_w_in_ab, v_gmlp_ln_g, v_gmlp_ln_b, v_gmlp_w_s, v_gmlp_b_s, v_conv_w, v_conv_b, v_conv_gn_g, v_conv_gn_b, v_w_out_ab, v_w_qkv_c, v_w_out_c, v_mem_wq, v_mem_wk, v_mem_wv, v_mem_wo, v_ffn_w1, v_ffn_w3, v_ffn_w2, v_ln_g, v_ln_b):
    w = dict(w_in_ab=w_in_ab, gmlp_ln_g=gmlp_ln_g, gmlp_ln_b=gmlp_ln_b, gmlp_w_s=gmlp_w_s, gmlp_b_s=gmlp_b_s,
             conv_w=conv_w, conv_b=conv_b, conv_gn_g=conv_gn_g, conv_gn_b=conv_gn_b, w_out_ab=w_out_ab,
             w_qkv_c=w_qkv_c, w_out_c=w_out_c, mem_wq=mem_wq, mem_wk=mem_wk, mem_wv=mem_wv, mem_wo=mem_wo,
             ffn_w1=ffn_w1, ffn_w3=ffn_w3, ffn_w2=ffn_w2, ln_g=ln_g, ln_b=ln_b)
    m = dict(w_in_ab=m_w_in_ab, gmlp_ln_g=m_gmlp_ln_g, gmlp_ln_b=m_gmlp_ln_b, gmlp_w_s=m_gmlp_w_s,
             gmlp_b_s=m_gmlp_b_s, conv_w=m_conv_w, conv_b=m_conv_b, conv_gn_g=m_conv_gn_g,
             conv_gn_b=m_conv_gn_b, w_out_ab=m_w_out_ab, w_qkv_c=m_w_qkv_c, w_out_c=m_w_out_c, mem_wq=m_mem_wq,
             mem_wk=m_mem_wk, mem_wv=m_mem_wv, mem_wo=m_mem_wo, ffn_w1=m_ffn_w1, ffn_w3=m_ffn_w3,
             ffn_w2=m_ffn_w2, ln_g=m_ln_g, ln_b=m_ln_b)
    v = dict(w_in_ab=v_w_in_ab, gmlp_ln_g=v_gmlp_ln_g, gmlp_ln_b=v_gmlp_ln_b, gmlp_w_s=v_gmlp_w_s,
             gmlp_b_s=v_gmlp_b_s, conv_w=v_conv_w, conv_b=v_conv_b, conv_gn_g=v_conv_gn_g,
             conv_gn_b=v_conv_gn_b, w_out_ab=v_w_out_ab, w_qkv_c=v_w_qkv_c, w_out_c=v_w_out_c, mem_wq=v_mem_wq,
             mem_wk=v_mem_wk, mem_wv=v_mem_wv, mem_wo=v_mem_wo, ffn_w1=v_ffn_w1, ffn_w3=v_ffn_w3,
             ffn_w2=v_ffn_w2, ln_g=v_ln_g, ln_b=v_ln_b)

    full = _gather_weights(w)
    loss_part, grad_x, grads = _local_step(x, mem, loss_target, full)
    loss = lax.psum(jnp.sum(loss_part), AXES)

    parts = _exchange(_pack_grads(grads), False, "exchange_grads")
    g_p, d_p, m_p, v_p = _adamw(parts, _pack_local(w), _pack_local(m), _pack_local(v))
    g_o, d_o, m_o, v_o = (_unpack_local(t, w) for t in (g_p, d_p, m_p, v_p))
    return (loss, grad_x, *[g_o[n] for n in _WEIGHTS], *[d_o[n] for n in _WEIGHTS],
            *[m_o[n] for n in _WEIGHTS], *[v_o[n] for n in _WEIGHTS])
```

```python
import functools

import jax
import jax.numpy as jnp
from jax import lax
from jax.experimental import pallas as pl
from jax.experimental.pallas import tpu as pltpu

F32 = jnp.float32
BF16 = jnp.bfloat16
_MXU = jnp.bfloat16

D_MODEL = 1024
DEPTH = 4
MEM_LEN = 256
CHUNK = 128
A_GROUPS = 4
A_WIDTH = 512
B_WIDTH = 512
GROUP_DIM = 128
CONV_WIDTH = 31
C_HEADS = 16
C_HEAD_DIM = 64
MEM_HEADS = 4
MEM_HEAD_DIM = 256
D_FF = 2816
ALPHA = (2.0 * DEPTH) ** 0.25
LN_EPS = 1e-5
SB_SCALE = C_HEAD_DIM ** -0.5
MEM_SCALE = MEM_HEAD_DIM ** -0.5

ADAM_LR = 0.001
ADAM_B1 = 0.9
ADAM_B2 = 0.999
ADAM_EPS = 1e-08
ADAM_WD = 0.01
ADAM_STEP = 10

N_DEV = 8
AXES = ("x", "y", "c")
LANE = 128
SUBLANE = 8
PACK_COLS = 1024
_VMEM_LIMIT = 56 * 1024 * 1024

SB_TQ = 256
SB_TK = 128
SB_HEADS = 2
SB_UNROLL = 2
D_FF_SHARD = D_FF // 8
D_FF_SHARD_PAD = 384
CONV_ROWS = 256
CONV_PAD = 32


def _tile(n, cap, mult):
    best = None
    for d in range(mult, min(n, cap) + 1, mult):
        if n % d == 0:
            best = d
    return n if best is None else best


def _call(body, *, name, grid, in_specs, out_specs, out_shape, scratch=(), sem=None):
    return pl.pallas_call(
        body, name=name, grid=grid, in_specs=in_specs, out_specs=out_specs, out_shape=out_shape,
        scratch_shapes=list(scratch),
        compiler_params=pltpu.CompilerParams(dimension_semantics=sem, vmem_limit_bytes=_VMEM_LIMIT))


def _dg(a, b, ca, cb):
    return lax.dot_general(a.astype(_MXU), b.astype(_MXU), (((ca,), (cb,)), ((), ())),
                           preferred_element_type=F32)


@jax.custom_vjp
def _dot(a, b):
    return _dg(a, b, 1, 0)


_dot.defvjp(lambda a, b: (_dg(a, b, 1, 0), (a, b)),
            lambda r, g: (_dg(g, r[1], 1, 1), _dg(r[0], g, 0, 0)))


@jax.custom_vjp
def _dot_t(a, b):
    return _dg(a, b, 1, 1)


_dot_t.defvjp(lambda a, b: (_dg(a, b, 1, 1), (a, b)),
              lambda r, g: (_dg(g, r[1], 1, 0), _dg(g, r[0], 0, 0)))


def _norm(z, g, b):
    mu = jnp.mean(z, axis=-1, keepdims=True)
    zc = z - mu
    var = jnp.mean(zc * zc, axis=-1, keepdims=True)
    return zc * lax.rsqrt(var + LN_EPS) * g + b


def _gelu(x):
    return 0.5 * x * (1.0 + lax.erf(x * (0.5 ** 0.5)))


def _matmul(a, b, mode="nn", add=None, out_dtype=F32, la=None, lb=None, name="mm"):
    ash, bsh = a.shape[-2:], b.shape[-2:]
    if mode == "nn":
        (M, K), (K2, N) = ash, bsh
    elif mode == "nt":
        (M, K), (N, K2) = ash, bsh
    else:
        (K, M), (K2, N) = ash, bsh
    assert K == K2, (a.shape, b.shape, mode)
    tm = _tile(M, 512, LANE)
    tn = _tile(N, 1408, LANE)
    tk = _tile(K, 512 if mode == "tn" else 1408, LANE)
    nk = K // tk
    if mode == "tn":
        a_blk, a_idx = (tk, tm), (lambda i, j, k: (k, i))
    else:
        a_blk, a_idx = (tm, tk), (lambda i, j, k: (i, k))
    if mode == "nt":
        b_blk, b_idx = (tn, tk), (lambda i, j, k: (j, k))
    else:
        b_blk, b_idx = (tk, tn), (lambda i, j, k: (k, j))
    dims = {"nn": (1, 0), "nt": (1, 1), "tn": (0, 0)}[mode]

    def spec(blk, idx, lead):
        if lead is None:
            return pl.BlockSpec(blk, idx)
        return pl.BlockSpec((None,) + blk, lambda i, j, k: (lead,) + idx(i, j, k))

    has_add = add is not None

    def body(*refs):
        a_ref, b_ref = refs[0], refs[1]
        add_ref = refs[2] if has_add else None
        o_ref = refs[3] if has_add else refs[2]
        p = _dg(a_ref[...], b_ref[...], *dims)

        def finish(r):
            if has_add:
                r = r + add_ref[...].astype(F32)
            o_ref[...] = r.astype(out_dtype)

        if nk == 1:
            finish(p)
        else:
            acc = refs[-1]
            k = pl.program_id(2)

            @pl.when(k == 0)
            def _():
                acc[...] = p

            @pl.when(k > 0)
            def _():
                acc[...] += p

            @pl.when(k == nk - 1)
            def _():
                finish(acc[...])

    in_specs = [spec(a_blk, a_idx, la), spec(b_blk, b_idx, lb)]
    args = [a, b]
    if has_add:
        in_specs.append(pl.BlockSpec((tm, tn), lambda i, j, k: (i, j)))
        args.append(add)
    return _call(
        body, name=name, grid=(M // tm, N // tn, nk), in_specs=in_specs,
        out_specs=pl.BlockSpec((tm, tn), lambda i, j, k: (i, j)),
        out_shape=jax.ShapeDtypeStruct((M, N), out_dtype),
        scratch=[pltpu.VMEM((tm, tn), F32)] if nk > 1 else [],
        sem=("parallel", "parallel", "arbitrary"))(*args)


def _ln_fn(x, f, g, b):
    return _norm(ALPHA * x + f, g, b)


def _ln_fwd(x, f, g, b):
    T, D = x.shape
    tm = _tile(T, 256, SUBLANE)

    def body(x_ref, f_ref, g_ref, b_ref, o_ref):
        o_ref[...] = _ln_fn(x_ref[...], f_ref[...], g_ref[...], b_ref[...])

    row = pl.BlockSpec((tm, D), lambda i: (i, 0))
    par = pl.BlockSpec((1, D), lambda i: (0, 0))
    return _call(body, name="ln_fwd", grid=(T // tm,), in_specs=[row, row, par, par], out_specs=row,
                 out_shape=jax.ShapeDtypeStruct((T, D), F32), sem=("parallel",))(x, f, g, b)


def _ln_bwd(x, f, g, b, dy):
    T, D = x.shape
    tm = _tile(T, 256, SUBLANE)

    def body(x_ref, f_ref, g_ref, b_ref, dy_ref, dx_ref, df_ref, dg_ref, db_ref):
        _, vjp = jax.vjp(_ln_fn, x_ref[...], f_ref[...], g_ref[...], b_ref[...])
        dx, df, dg, db = vjp(dy_ref[...])
        dx_ref[...] = dx
        df_ref[...] = df.astype(BF16)

        @pl.when(pl.program_id(0) == 0)
        def _():
            dg_ref[...] = jnp.zeros_like(dg_ref)
            db_ref[...] = jnp.zeros_like(db_ref)

        dg_ref[...] += dg
        db_ref[...] += db

    row = pl.BlockSpec((tm, D), lambda i: (i, 0))
    par = pl.BlockSpec((1, D), lambda i: (0, 0))
    return _call(body, name="ln_bwd", grid=(T // tm,), in_specs=[row, row, par, par, row],
                 out_specs=[row, row, par, par],
                 out_shape=[jax.ShapeDtypeStruct((T, D), F32), jax.ShapeDtypeStruct((T, D), BF16),
                            jax.ShapeDtypeStruct((1, D), F32), jax.ShapeDtypeStruct((1, D), F32)],
                 sem=("arbitrary",))(x, f, g, b, dy)


def _loss_fwd(y, tgt):
    T, D = y.shape
    tm = _tile(T, 256, SUBLANE)

    def body(y_ref, t_ref, l_ref, dy_ref):
        d = y_ref[...] - t_ref[...]
        dy_ref[...] = d * (1.0 / D)

        @pl.when(pl.program_id(0) == 0)
        def _():
            l_ref[...] = jnp.zeros_like(l_ref)

        l_ref[...] += jnp.sum(d * d, axis=0, keepdims=True) * (0.5 / D)

    row = pl.BlockSpec((tm, D), lambda i: (i, 0))
    par = pl.BlockSpec((1, D), lambda i: (0, 0))
    return _call(body, name="loss", grid=(T // tm,), in_specs=[row, row], out_specs=[par, row],
                 out_shape=[jax.ShapeDtypeStruct((1, D), F32), jax.ShapeDtypeStruct((T, D), F32)],
                 sem=("arbitrary",))(y, tgt)


def _swiglu_fn(h1, h3):
    return h1 * jax.nn.sigmoid(h1) * h3


def _swiglu_fwd(h13):
    T, F2 = h13.shape
    F = F2 // 2
    tm = _tile(T, 256, SUBLANE)

    def body(h1_ref, h3_ref, o_ref):
        o_ref[...] = _swiglu_fn(h1_ref[...], h3_ref[...]).astype(BF16)

    return _call(body, name="swiglu_fwd", grid=(T // tm,),
                 in_specs=[pl.BlockSpec((tm, F), lambda i: (i, 0)), pl.BlockSpec((tm, F), lambda i: (i, 1))],
                 out_specs=pl.BlockSpec((tm, F), lambda i: (i, 0)),
                 out_shape=jax.ShapeDtypeStruct((T, F), BF16), sem=("parallel",))(h13, h13)


def _swiglu_bwd(h13, dact):
    T, F2 = h13.shape
    F = F2 // 2
    tm = _tile(T, 256, SUBLANE)

    def body(h1_ref, h3_ref, d_ref, o_ref):
        _, vjp = jax.vjp(_swiglu_fn, h1_ref[...], h3_ref[...])
        d1, d3 = vjp(d_ref[...])
        o_ref[:, :F] = d1.astype(BF16)
        o_ref[:, F:] = d3.astype(BF16)

    return _call(body, name="swiglu_bwd", grid=(T // tm,),
                 in_specs=[pl.BlockSpec((tm, F), lambda i: (i, 0)), pl.BlockSpec((tm, F), lambda i: (i, 1)),
                           pl.BlockSpec((tm, F), lambda i: (i, 0))],
                 out_specs=pl.BlockSpec((tm, F2), lambda i: (i, 0)),
                 out_shape=jax.ShapeDtypeStruct((T, F2), BF16), sem=("parallel",))(h13, h13, dact)


@jax.custom_vjp
def _chunkmix(wm, vn, bs):
    n = vn.shape[0] // CHUNK
    return jnp.concatenate([_dg(wm, vn[c * CHUNK:(c + 1) * CHUNK], 1, 0) + bs for c in range(n)], axis=0)


def _chunkmix_fwd(wm, vn, bs):
    return _chunkmix(wm, vn, bs), (wm, vn)


def _chunkmix_bwd(res, ct):
    wm, vn = res
    n = vn.shape[0] // CHUNK
    cts = [ct[c * CHUNK:(c + 1) * CHUNK] for c in range(n)]
    dvn = jnp.concatenate([_dg(wm, cts[c], 0, 0) for c in range(n)], axis=0)
    dwm = sum(_dg(cts[c], vn[c * CHUNK:(c + 1) * CHUNK], 1, 1) for c in range(n))
    dbs = sum(jnp.sum(cts[c], axis=1, keepdims=True) for c in range(n))
    return dwm, dvn, dbs


_chunkmix.defvjp(_chunkmix_fwd, _chunkmix_bwd)


def _gmlp_fn(u, v, lg, lb, ws, bs):
    ug = _gelu(u)
    vn = _norm(_gelu(v), lg, lb)
    r = lax.broadcasted_iota(jnp.int32, (CHUNK, CHUNK), 0)
    c = lax.broadcasted_iota(jnp.int32, (CHUNK, CHUNK), 1)
    wm = jnp.where(r >= c, ws, 0.0)
    return ug * _chunkmix(wm, vn, bs)


def _gmlp_specs(tm, order):
    def at(col0):
        return pl.BlockSpec((tm, GROUP_DIM), lambda *ids: (order(*ids)[0], col0 + order(*ids)[1]))
    par = pl.BlockSpec((1, GROUP_DIM), lambda *ids: (0, order(*ids)[1]))
    ws = pl.BlockSpec((None, CHUNK, CHUNK), lambda *ids: (order(*ids)[1], 0, 0))
    bs = pl.BlockSpec((None, CHUNK, 1), lambda *ids: (order(*ids)[1], 0, 0))
    return at, par, ws, bs


def _gmlp_fwd(hin, lg, lb, ws, bs):
    T = hin.shape[0]
    tm = _tile(T, 512, CHUNK)
    at, par, wsp, bsp = _gmlp_specs(tm, lambda i, g: (i, g))

    def body(u_ref, v_ref, lg_ref, lb_ref, ws_ref, bs_ref, o_ref):
        o_ref[...] = _gmlp_fn(u_ref[...], v_ref[...], lg_ref[...], lb_ref[...], ws_ref[...],
                              bs_ref[...]).astype(BF16)

    return _call(body, name="gmlp_fwd", grid=(T // tm, A_GROUPS),
                 in_specs=[at(0), at(A_GROUPS), par, par, wsp, bsp], out_specs=at(0),
                 out_shape=jax.ShapeDtypeStruct((T, A_WIDTH), BF16),
                 sem=("parallel", "parallel"))(hin, hin, lg, lb, ws, bs)


def _gmlp_bwd(hin, lg, lb, ws, bs, dyab):
    T = hin.shape[0]
    tm = _tile(T, 512, CHUNK)
    at, par, wsp, bsp = _gmlp_specs(tm, lambda g, i: (i, g))

    def body(u_ref, v_ref, lg_ref, lb_ref, ws_ref, bs_ref, dy_ref,
             du_ref, dv_ref, dlg_ref, dlb_ref, dws_ref, dbs_ref):
        _, vjp = jax.vjp(_gmlp_fn, u_ref[...], v_ref[...], lg_ref[...], lb_ref[...], ws_ref[...], bs_ref[...])
        du, dv, dlg, dlb, dws, dbs = vjp(dy_ref[...])
        du_ref[...] = du.astype(BF16)
        dv_ref[...] = dv.astype(BF16)

        @pl.when(pl.program_id(1) == 0)
        def _():
            dlg_ref[...] = jnp.zeros_like(dlg_ref)
            dlb_ref[...] = jnp.zeros_like(dlb_ref)
            dws_ref[...] = jnp.zeros_like(dws_ref)
            dbs_ref[...] = jnp.zeros_like(dbs_ref)

        dlg_ref[...] += dlg
        dlb_ref[...] += dlb
        dws_ref[...] += dws
        dbs_ref[...] += dbs

    half = jax.ShapeDtypeStruct((T, A_WIDTH), BF16)
    return _call(body, name="gmlp_bwd", grid=(A_GROUPS, T // tm),
                 in_specs=[at(0), at(A_GROUPS), par, par, wsp, bsp, at(0)],
                 out_specs=[at(0), at(0), par, par, wsp, bsp],
                 out_shape=[half, half, jax.ShapeDtypeStruct((1, A_WIDTH), F32),
                            jax.ShapeDtypeStruct((1, A_WIDTH), F32),
                            jax.ShapeDtypeStruct((A_GROUPS, CHUNK, CHUNK), F32),
                            jax.ShapeDtypeStruct((A_GROUPS, CHUNK, 1), F32)],
                 sem=("parallel", "arbitrary"))(hin, hin, lg, lb, ws, bs, dyab)


def _glu_fn(a, gt):
    return a * jax.nn.sigmoid(gt)


def _gn_silu_fn(c, cb, gg, gb):
    y = _norm(c + cb, gg, gb)
    return y * jax.nn.sigmoid(y)


def _conv_taps(w_ref, src_ref, row0, first):
    acc = None
    for k in range(CONV_WIDTH):
        term = w_ref[pl.ds(k, 1), :] * src_ref[pl.ds(row0 + first(k), CONV_ROWS), :]
        acc = term if acc is None else acc + term
    return acc


def _conv_specs(S, order):
    def at(col0):
        return pl.BlockSpec((S, GROUP_DIM), lambda *ids: (order(*ids)[0], col0 + order(*ids)[1]))
    par = pl.BlockSpec((1, GROUP_DIM), lambda *ids: (0, order(*ids)[1]))
    cw = pl.BlockSpec((CONV_WIDTH, GROUP_DIM), lambda *ids: (0, order(*ids)[1]))
    return at, par, cw


def _conv_fwd(hin, cw, cb, gg, gb, n_ex, S):
    T = hin.shape[0]
    at, par, cwp = _conv_specs(S, lambda e, g: (e, g))
    ngrp = B_WIDTH // GROUP_DIM
    lead = CONV_PAD - (CONV_WIDTH - 1)

    def body(a_ref, gt_ref, cw_ref, cb_ref, gg_ref, gb_ref, o_ref, pad_ref):
        pad_ref[0:CONV_PAD, :] = jnp.zeros((CONV_PAD, GROUP_DIM), F32)
        for r in range(S // CONV_ROWS):
            rows = pl.ds(r * CONV_ROWS, CONV_ROWS)
            pad_ref[pl.ds(CONV_PAD + r * CONV_ROWS, CONV_ROWS), :] = _glu_fn(a_ref[rows, :], gt_ref[rows, :])
        for r in range(S // CONV_ROWS):
            c = _conv_taps(cw_ref, pad_ref, r * CONV_ROWS, lambda k: lead + k)
            o_ref[pl.ds(r * CONV_ROWS, CONV_ROWS), :] = _gn_silu_fn(
                c, cb_ref[...], gg_ref[...], gb_ref[...]).astype(BF16)

    return _call(body, name="conv_fwd", grid=(n_ex, ngrp),
                 in_specs=[at(2 * A_GROUPS), at(2 * A_GROUPS + ngrp), cwp, par, par, par], out_specs=at(0),
                 out_shape=jax.ShapeDtypeStruct((T, B_WIDTH), BF16),
                 scratch=[pltpu.VMEM((S + CONV_PAD, GROUP_DIM), F32)],
                 sem=("parallel", "parallel"))(hin, hin, cw, cb, gg, gb)


def _conv_bwd(hin, cw, cb, gg, gb, dyab, n_ex, S):
    T = hin.shape[0]
    at, par, cwp = _conv_specs(S, lambda g, e: (e, g))
    ngrp = B_WIDTH // GROUP_DIM
    lead = CONV_PAD - (CONV_WIDTH - 1)
    nchunk = S // CONV_ROWS

    def body(a_ref, gt_ref, cw_ref, cb_ref, gg_ref, gb_ref, dy_ref,
             da_ref, dgt_ref, dcw_ref, dcb_ref, dgg_ref, dgb_ref, pad_ref, dcp_ref, wacc_ref, pacc_ref):
        @pl.when(pl.program_id(1) == 0)
        def _():
            wacc_ref[...] = jnp.zeros_like(wacc_ref)
            pacc_ref[...] = jnp.zeros_like(pacc_ref)

        pad_ref[0:CONV_PAD, :] = jnp.zeros((CONV_PAD, GROUP_DIM), F32)
        dcp_ref[S:S + CONV_PAD, :] = jnp.zeros((CONV_PAD, GROUP_DIM), F32)
        for r in range(nchunk):
            rows = pl.ds(r * CONV_ROWS, CONV_ROWS)
            pad_ref[pl.ds(CONV_PAD + r * CONV_ROWS, CONV_ROWS), :] = _glu_fn(a_ref[rows, :], gt_ref[rows, :])
        for r in range(nchunk):
            rows = pl.ds(r * CONV_ROWS, CONV_ROWS)
            c = _conv_taps(cw_ref, pad_ref, r * CONV_ROWS, lambda k: lead + k)
            _, vjp = jax.vjp(_gn_silu_fn, c, cb_ref[...], gg_ref[...], gb_ref[...])
            dc, dcb, dgg, dgb = vjp(dy_ref[rows, :])
            dcp_ref[rows, :] = dc
            pacc_ref[0:1, :] += dcb
            pacc_ref[1:2, :] += dgg
            pacc_ref[2:3, :] += dgb
        for r in range(nchunk):
            rows = pl.ds(r * CONV_ROWS, CONV_ROWS)
            dh = _conv_taps(cw_ref, dcp_ref, r * CONV_ROWS, lambda k: CONV_WIDTH - 1 - k)
            _, vjp = jax.vjp(_glu_fn, a_ref[rows, :], gt_ref[rows, :])
            da, dgt = vjp(dh)
            da_ref[rows, :] = da.astype(BF16)
            dgt_ref[rows, :] = dgt.astype(BF16)
            dc = dcp_ref[rows, :]
            for k in range(CONV_WIDTH):
                prod = dc * pad_ref[pl.ds(r * CONV_ROWS + lead + k, CONV_ROWS), :]
                wacc_ref[k] += jnp.sum(prod.reshape(CONV_ROWS // SUBLANE, SUBLANE, GROUP_DIM), axis=0)
        for k in range(CONV_WIDTH):
            dcw_ref[pl.ds(k, 1), :] = jnp.sum(wacc_ref[k], axis=0, keepdims=True)
        dcb_ref[...] = pacc_ref[0:1, :]
        dgg_ref[...] = pacc_ref[1:2, :]
        dgb_ref[...] = pacc_ref[2:3, :]

    half = jax.ShapeDtypeStruct((T, B_WIDTH), BF16)
    vec = jax.ShapeDtypeStruct((1, B_WIDTH), F32)
    return _call(body, name="conv_bwd", grid=(ngrp, n_ex),
                 in_specs=[at(2 * A_GROUPS), at(2 * A_GROUPS + ngrp), cwp, par, par, par, at(ngrp)],
                 out_specs=[at(0), at(0), cwp, par, par, par],
                 out_shape=[half, half, jax.ShapeDtypeStruct((CONV_WIDTH, B_WIDTH), F32), vec, vec, vec],
                 scratch=[pltpu.VMEM((S + CONV_PAD, GROUP_DIM), F32), pltpu.VMEM((S + CONV_PAD, GROUP_DIM), F32),
                          pltpu.VMEM((CONV_WIDTH, SUBLANE, GROUP_DIM), F32), pltpu.VMEM((SUBLANE, GROUP_DIM), F32)],
                 sem=("parallel", "arbitrary"))(hin, hin, cw, cb, gg, gb, dyab)


def _sb_consts():
    r = lax.broadcasted_iota(jnp.int32, (SB_TK + SUBLANE, 2 * SB_TK), 0)
    c = lax.broadcasted_iota(jnp.int32, (SB_TK + SUBLANE, 2 * SB_TK), 1) % SB_TK
    tail = r >= SB_TK
    u_after = jnp.where((c > r) | tail, 1.0, 0.0).astype(BF16)
    u_before = jnp.where((c < r) | tail, 1.0, 0.0).astype(BF16)
    s = lax.broadcasted_iota(jnp.int32, (SB_TK, SB_TQ), 0)
    t = lax.broadcasted_iota(jnp.int32, (SB_TK, SB_TQ), 1)
    masks = [(s + SB_TK * d) < t for d in range(SB_TQ // SB_TK)]
    return u_after, u_before, masks


def _split_sum(u, x):
    hi = x.astype(BF16)
    lo = (x - hi.astype(F32)).astype(BF16)
    res = _dg(u, jnp.concatenate([hi, lo], axis=0), 1, 0)
    return res[:SB_TK], res[SB_TK:]


def _add8(x, row8):
    return (x.reshape(-1, SUBLANE, x.shape[-1]) + row8[None]).reshape(x.shape)


def _sb_scores(kj, qi, mask):
    z = _dg(kj, qi, 0, 0) * SB_SCALE
    e = jnp.exp(-jnp.abs(z))
    sp = jnp.maximum(z, 0.0) + jnp.log1p(e)
    lognot = -sp if mask is None else jnp.where(mask, -sp, 0.0)
    return z, e, sp, lognot


def _sb_blocked(dst, src, h, n, width):
    for t in range(n):
        dst[h, t] = src[h, :, t * width:(t + 1) * width].astype(dst.dtype)


def _sb_spec(S):
    def at(part):
        return pl.BlockSpec((SB_HEADS, C_HEAD_DIM, S), lambda e, h: (part * (C_HEADS // SB_HEADS) + h, 0, e))
    return at


def _sb_fwd(qkvT, n_ex, S):
    T = qkvT.shape[-1]
    nq, nk = S // SB_TQ, S // SB_TK
    per = SB_TQ // SB_TK
    assert per % SB_UNROLL == 0
    heads = range(SB_HEADS)

    def body(q_ref, k_ref, v_ref, o_ref, qb, kb, vb, ob):
        for h in heads:
            _sb_blocked(qb, q_ref, h, nq, SB_TQ)
            _sb_blocked(kb, k_ref, h, nk, SB_TK)
            _sb_blocked(vb, v_ref, h, nk, SB_TK)
        u_after, _, masks = _sb_consts()

        def tile(h, j, qi, carry, acc, mask):
            z, _, sp, lognot = _sb_scores(kb[h, j], qi, mask)
            within, total = _split_sum(u_after, lognot)
            att = jnp.exp((z - sp) + _add8(within, carry))
            if mask is not None:
                att = jnp.where(mask, att, 0.0)
            return carry + total, acc + _dg(vb[h, j], att, 1, 0)

        def qtile(i, _):
            qs = [qb[h, i] for h in heads]
            st = tuple((jnp.zeros((SUBLANE, SB_TQ), F32), jnp.zeros((C_HEAD_DIM, SB_TQ), F32)) for h in heads)
            for d in reversed(range(per)):
                st = tuple(tile(h, per * i + d, qs[h], *st[h], masks[d]) for h in heads)

            def below(jj, st):
                for u in range(SB_UNROLL):
                    j = per * i - 1 - (SB_UNROLL * jj + u)
                    st = tuple(tile(h, j, qs[h], *st[h], None) for h in heads)
                return st

            st = lax.fori_loop(0, per * i // SB_UNROLL, below, st)
            for h in heads:
                ob[h, i] = st[h][1]
            return 0

        lax.fori_loop(0, nq, qtile, 0)
        for h in heads:
            for i in range(nq):
                o_ref[h, :, i * SB_TQ:(i + 1) * SB_TQ] = ob[h, i].astype(BF16)

    at = _sb_spec(S)
    return _call(body, name="sb_fwd", grid=(n_ex, C_HEADS // SB_HEADS), in_specs=[at(0), at(1), at(2)],
                 out_specs=at(0), out_shape=jax.ShapeDtypeStruct((C_HEADS, C_HEAD_DIM, T), BF16),
                 scratch=[pltpu.VMEM((SB_HEADS, nq, C_HEAD_DIM, SB_TQ), _MXU),
                          pltpu.VMEM((SB_HEADS, nk, C_HEAD_DIM, SB_TK), _MXU),
                          pltpu.VMEM((SB_HEADS, nk, C_HEAD_DIM, SB_TK), _MXU),
                          pltpu.VMEM((SB_HEADS, nq, C_HEAD_DIM, SB_TQ), F32)],
                 sem=("parallel", "parallel"))(qkvT, qkvT, qkvT)


def _sb_bwd(qkvT, doT, n_ex, S):
    T = qkvT.shape[-1]
    nq, nk = S // SB_TQ, S // SB_TK
    per = SB_TQ // SB_TK

    assert per % SB_UNROLL == 0
    heads = range(SB_HEADS)

    def body(q_ref, k_ref, v_ref, do_ref, dq_ref, dk_ref, dv_ref, qb, kb, vb, dob, dqa, dka, dva, dl_s, sg_s):
        for h in heads:
            _sb_blocked(qb, q_ref, h, nq, SB_TQ)
            _sb_blocked(dob, do_ref, h, nq, SB_TQ)
            _sb_blocked(kb, k_ref, h, nk, SB_TK)
            _sb_blocked(vb, v_ref, h, nk, SB_TK)
        dka[...] = jnp.zeros_like(dka)
        dva[...] = jnp.zeros_like(dva)
        u_after, u_before, masks = _sb_consts()

        def rebuild(h, j, qi, gi, carry, mask):
            z, e, sp, lognot = _sb_scores(kb[h, j], qi, mask)
            within, total = _split_sum(u_after, lognot)
            att = jnp.exp((z - sp) + _add8(within, carry))
            r = 1.0 / (1.0 + e)
            sig = jnp.where(z >= 0.0, r, e * r)
            if mask is not None:
                att = jnp.where(mask, att, 0.0)
                sig = jnp.where(mask, sig, 0.0)
            dl_s[h, j] = _dg(vb[h, j], gi, 0, 0) * att
            sg_s[h, j] = sig
            dva[h, j] += _dg(gi, att, 1, 1)
            return carry + total

        def push(h, j, qi, carry, dq):
            dlogit, sig = dl_s[h, j], sg_s[h, j]
            within, total = _split_sum(u_before, dlogit)
            dz = (dlogit * (1.0 - sig) - sig * _add8(within, carry)) * SB_SCALE
            dka[h, j] += _dg(qi, dz, 1, 1)
            return carry + total, dq + _dg(kb[h, j], dz, 1, 0)

        def qtile(i, _):
            qs = [qb[h, i] for h in heads]
            gs = [dob[h, i] for h in heads]
            carry = tuple(jnp.zeros((SUBLANE, SB_TQ), F32) for h in heads)
            for d in reversed(range(per)):
                carry = tuple(rebuild(h, per * i + d, qs[h], gs[h], carry[h], masks[d]) for h in heads)

            def below(jj, carry):
                for u in range(SB_UNROLL):
                    j = per * i - 1 - (SB_UNROLL * jj + u)
                    carry = tuple(rebuild(h, j, qs[h], gs[h], carry[h], None) for h in heads)
                return carry

            lax.fori_loop(0, per * i // SB_UNROLL, below, carry)

            def upward(jj, st):
                for u in range(SB_UNROLL):
                    st = tuple(push(h, SB_UNROLL * jj + u, qs[h], *st[h]) for h in heads)
                return st

            st = tuple((jnp.zeros((SUBLANE, SB_TQ), F32), jnp.zeros((C_HEAD_DIM, SB_TQ), F32)) for h in heads)
            st = lax.fori_loop(0, per * (i + 1) // SB_UNROLL, upward, st)
            for h in heads:
                dqa[h, i] = st[h][1]
            return 0

        lax.fori_loop(0, nq, qtile, 0)
        for h in heads:
            for i in range(nq):
                dq_ref[h, :, i * SB_TQ:(i + 1) * SB_TQ] = dqa[h, i].astype(BF16)
            for j in range(nk):
                dk_ref[h, :, j * SB_TK:(j + 1) * SB_TK] = dka[h, j].astype(BF16)
                dv_ref[h, :, j * SB_TK:(j + 1) * SB_TK] = dva[h, j].astype(BF16)

    at = _sb_spec(S)
    out = jax.ShapeDtypeStruct((C_HEADS, C_HEAD_DIM, T), BF16)
    qshape, kshape = (SB_HEADS, nq, C_HEAD_DIM, SB_TQ), (SB_HEADS, nk, C_HEAD_DIM, SB_TK)
    return _call(body, name="sb_bwd", grid=(n_ex, C_HEADS // SB_HEADS), in_specs=[at(0), at(1), at(2), at(0)],
                 out_specs=[at(0), at(0), at(0)], out_shape=[out, out, out],
                 scratch=[pltpu.VMEM(qshape, _MXU), pltpu.VMEM(kshape, _MXU), pltpu.VMEM(kshape, _MXU),
                          pltpu.VMEM(qshape, _MXU), pltpu.VMEM(qshape, F32), pltpu.VMEM(kshape, F32),
                          pltpu.VMEM(kshape, F32), pltpu.VMEM((SB_HEADS, nk, SB_TK, SB_TQ), F32),
                          pltpu.VMEM((SB_HEADS, nk, SB_TK, SB_TQ), F32)],
                 sem=("parallel", "parallel"))(qkvT, qkvT, qkvT, doT)


def _xattn_fn(q, k, v):
    s = _dot_t(q, k) * MEM_SCALE
    m = lax.stop_gradient(jnp.max(s, axis=-1, keepdims=True))
    p = jnp.exp(s - m)
    p = p / jnp.sum(p, axis=-1, keepdims=True)
    return _dot(p, v)


def _xattn_fwd(q, kk, vv, n_ex, S):
    T = q.shape[0]
    tq = _tile(S, 1024, SUBLANE)
    nqt = S // tq

    def body(q_ref, k_ref, v_ref, o_ref):
        o_ref[...] = _xattn_fn(q_ref[...], k_ref[...], v_ref[...]).astype(BF16)

    qs = pl.BlockSpec((tq, MEM_HEAD_DIM), lambda e, i, h: (e * nqt + i, h))
    ks = pl.BlockSpec((MEM_LEN, MEM_HEAD_DIM), lambda e, i, h: (e, h))
    return _call(body, name="xattn_fwd", grid=(n_ex, nqt, MEM_HEADS), in_specs=[qs, ks, ks], out_specs=qs,
                 out_shape=jax.ShapeDtypeStruct((T, D_MODEL), BF16),
                 sem=("parallel", "parallel", "parallel"))(q, kk, vv)


def _xattn_bwd(q, kk, vv, do, n_ex, S):
    T = q.shape[0]
    tq = _tile(S, 1024, SUBLANE)
    nqt = S // tq

    def body(q_ref, k_ref, v_ref, do_ref, dq_ref, dk_ref, dv_ref):
        _, vjp = jax.vjp(_xattn_fn, q_ref[...].astype(F32), k_ref[...].astype(F32), v_ref[...].astype(F32))
        dq, dk, dv = vjp(do_ref[...].astype(F32))
        dq_ref[...] = dq.astype(BF16)

        @pl.when(pl.program_id(2) == 0)
        def _():
            dk_ref[...] = jnp.zeros_like(dk_ref)
            dv_ref[...] = jnp.zeros_like(dv_ref)

        dk_ref[...] += dk
        dv_ref[...] += dv

    qs = pl.BlockSpec((tq, MEM_HEAD_DIM), lambda e, h, i: (e * nqt + i, h))
    ks = pl.BlockSpec((MEM_LEN, MEM_HEAD_DIM), lambda e, h, i: (e, h))
    kv = jax.ShapeDtypeStruct(kk.shape, F32)
    return _call(body, name="xattn_bwd", grid=(n_ex, MEM_HEADS, nqt), in_specs=[qs, ks, ks, qs],
                 out_specs=[qs, ks, ks], out_shape=[jax.ShapeDtypeStruct((T, D_MODEL), BF16), kv, kv],
                 sem=("parallel", "parallel", "arbitrary"))(q, kk, vv, do)


def _local_step(x, mem, tgt, W):
    n_ex, S, D = x.shape
    T = n_ex * S
    h = x.reshape(T, D)
    mem2 = mem.reshape(n_ex * MEM_LEN, D)
    row = lambda v: v.reshape(1, -1)
    saved = []
    for l in range(DEPTH):
        sv = dict(x0=h)
        if l % 2 == 0:
            e = l // 2
            hin = _matmul(h, W["w_in_ab"], lb=e, name="in_proj")
            ya = _gmlp_fwd(hin, row(W["gmlp_ln_g"][e]), row(W["gmlp_ln_b"][e]), W["gmlp_w_s"][e],
                           W["gmlp_b_s"][e][:, :, None])
            yb = _conv_fwd(hin, W["conv_w"][e], row(W["conv_b"][e]), row(W["conv_gn_g"][e]),
                           row(W["conv_gn_b"][e]), n_ex, S)
            yab = jnp.concatenate([ya, yb], axis=1)
            mix = _matmul(yab, W["w_out_ab"], lb=e, name="out_proj")
            sv.update(hin=hin, yab=yab)
        else:
            o = l // 2
            qkvT = _matmul(W["w_qkv_cT"], h, mode="nt", la=o, out_dtype=BF16, name="qkv_proj")
            yT = _sb_fwd(qkvT.reshape(3 * C_HEADS, C_HEAD_DIM, T), n_ex, S)
            mix = _matmul(yT.reshape(D, T), W["w_out_c"], mode="tn", lb=o, name="sb_out_proj")
            sv.update(qkvT=qkvT, yT=yT)
        x1 = _ln_fwd(h, mix, row(W["ln_g"][l, 0]), row(W["ln_b"][l, 0]))
        q = _matmul(x1, W["mem_wq"], lb=l, out_dtype=BF16, name="mem_q")
        kk = _matmul(mem2, W["mem_wk"], lb=l, out_dtype=BF16, name="mem_kv")
        vv = _matmul(mem2, W["mem_wv"], lb=l, out_dtype=BF16, name="mem_kv")
        oc = _xattn_fwd(q, kk, vv, n_ex, S)
        cross = _matmul(oc, W["mem_wo"], lb=l, name="mem_o")
        x2 = _ln_fwd(x1, cross, row(W["ln_g"][l, 1]), row(W["ln_b"][l, 1]))
        h13 = _matmul(x2, W["ffn_w13"], lb=l, name="ffn_in")
        act = _swiglu_fwd(h13)
        f = _matmul(act, W["ffn_w2"], lb=l, name="ffn_out")
        x3 = _ln_fwd(x2, f, row(W["ln_g"][l, 2]), row(W["ln_b"][l, 2]))
        sv.update(mix=mix, x1=x1, q=q, kk=kk, vv=vv, oc=oc, cross=cross, x2=x2, h13=h13, act=act, f=f)
        saved.append(sv)
        h = x3

    loss_part, dh = _loss_fwd(h, tgt.reshape(T, D))

    G = {k: [None] * n for k, n in [
        ("w_in_ab", 2), ("gmlp_ln_g", 2), ("gmlp_ln_b", 2), ("gmlp_w_s", 2), ("gmlp_b_s", 2), ("conv_w", 2),
        ("conv_b", 2), ("conv_gn_g", 2), ("conv_gn_b", 2), ("w_out_ab", 2), ("w_qkv_c", 2), ("w_out_c", 2),
        ("mem_wq", 4), ("mem_wk", 4), ("mem_wv", 4), ("mem_wo", 4), ("ffn_w1", 4), ("ffn_w3", 4),
        ("ffn_w2", 4), ("ln_g", 4), ("ln_b", 4)]}
    for l in reversed(range(DEPTH)):
        sv = saved[l]
        lng, lnb = [None] * 3, [None] * 3
        dx2, df, lng[2], lnb[2] = _ln_bwd(sv["x2"], sv["f"], row(W["ln_g"][l, 2]), row(W["ln_b"][l, 2]), dh)
        dact = _matmul(df, W["ffn_w2"], mode="nt", lb=l, name="ffn_out_dx")
        G["ffn_w2"][l] = _matmul(sv["act"], df, mode="tn", name="ffn_out_dw")
        dh13 = _swiglu_bwd(sv["h13"], dact)
        dx2 = _matmul(dh13, W["ffn_w13"], mode="nt", lb=l, add=dx2, name="ffn_in_dx")
        dw13 = _matmul(sv["x2"], dh13, mode="tn", name="ffn_in_dw")
        ff = dw13.shape[1] // 2
        G["ffn_w1"][l], G["ffn_w3"][l] = dw13[:, :ff], dw13[:, ff:]
        dx1, dcross, lng[1], lnb[1] = _ln_bwd(sv["x1"], sv["cross"], row(W["ln_g"][l, 1]),
                                              row(W["ln_b"][l, 1]), dx2)
        doc = _matmul(dcross, W["mem_wo"], mode="nt", lb=l, out_dtype=BF16, name="mem_o_dx")
        G["mem_wo"][l] = _matmul(sv["oc"], dcross, mode="tn", name="mem_o_dw")
        dq, dkk, dvv = _xattn_bwd(sv["q"], sv["kk"], sv["vv"], doc, n_ex, S)
        dx1 = _matmul(dq, W["mem_wq"], mode="nt", lb=l, add=dx1, name="mem_q_dx")
        G["mem_wq"][l] = _matmul(sv["x1"], dq, mode="tn", name="mem_q_dw")
        G["mem_wk"][l] = _matmul(mem2, dkk, mode="tn", name="mem_kv_dw")
        G["mem_wv"][l] = _matmul(mem2, dvv, mode="tn", name="mem_kv_dw")
        dx0, dmix, lng[0], lnb[0] = _ln_bwd(sv["x0"], sv["mix"], row(W["ln_g"][l, 0]), row(W["ln_b"][l, 0]), dx1)
        if l % 2 == 0:
            e = l // 2
            dyab = _matmul(dmix, W["w_out_ab"], mode="nt", lb=e, name="out_proj_dx")
            G["w_out_ab"][e] = _matmul(sv["yab"], dmix, mode="tn", name="out_proj_dw")
            du, dv, dlg, dlb, dws, dbs = _gmlp_bwd(
                sv["hin"], row(W["gmlp_ln_g"][e]), row(W["gmlp_ln_b"][e]), W["gmlp_w_s"][e],
                W["gmlp_b_s"][e][:, :, None], dyab)
            da, dgt, dcw, dcb, dgg, dgb = _conv_bwd(
                sv["hin"], W["conv_w"][e], row(W["conv_b"][e]), row(W["conv_gn_g"][e]),
                row(W["conv_gn_b"][e]), dyab, n_ex, S)
            dhin = jnp.concatenate([du, dv, da, dgt], axis=1)
            dh = _matmul(dhin, W["w_in_ab"], mode="nt", lb=e, add=dx0, name="in_proj_dx")
            G["w_in_ab"][e] = _matmul(sv["x0"], dhin, mode="tn", name="in_proj_dw")
            G["gmlp_ln_g"][e], G["gmlp_ln_b"][e] = dlg[0], dlb[0]
            G["gmlp_w_s"][e], G["gmlp_b_s"][e] = dws, dbs[:, :, 0]
            G["conv_w"][e], G["conv_b"][e], G["conv_gn_g"][e], G["conv_gn_b"][e] = dcw, dcb[0], dgg[0], dgb[0]
        else:
            o = l // 2
            dyT = _matmul(W["w_out_c"], dmix, mode="nt", la=o, out_dtype=BF16, name="sb_out_proj_dx")
            G["w_out_c"][o] = _matmul(sv["yT"].reshape(D, T), dmix, name="sb_out_proj_dw")
            dqT, dkT, dvT = _sb_bwd(sv["qkvT"].reshape(3 * C_HEADS, C_HEAD_DIM, T),
                                    dyT.reshape(C_HEADS, C_HEAD_DIM, T), n_ex, S)
            dqkvT = jnp.concatenate([dqT, dkT, dvT], axis=0).reshape(3 * D, T)
            dh = _matmul(dqkvT, W["w_qkv_cT"], mode="tn", lb=o, add=dx0, name="qkv_proj_dx")
            G["w_qkv_c"][o] = _matmul(dqkvT, sv["x0"], name="qkv_proj_dw").T
        G["ln_g"][l] = jnp.concatenate(lng, axis=0)
        G["ln_b"][l] = jnp.concatenate(lnb, axis=0)
    grads = {k: jnp.stack(v, axis=0) for k, v in G.items()}
    return loss_part, dh.reshape(n_ex, S, D), grads


def _exchange(p, bcast, name):
    shape = p.shape[-2:]

    def body(p_ref, out_ref, send_sems, recv_sems, local_sem):
        pos = [lax.axis_index(a) for a in AXES]
        me = 4 * pos[0] + 2 * pos[1] + pos[2]

        def src(idx):
            return p_ref if bcast else p_ref.at[idx]

        local = pltpu.make_async_copy(src(me), out_ref.at[me], local_sem)
        local.start()
        sends, recvs = [], []
        for k in range(1, N_DEV):
            bits = ((k >> 2) & 1, (k >> 1) & 1, k & 1)
            peer = tuple(1 - c if b else c for c, b in zip(pos, bits))
            peer_idx = 4 * peer[0] + 2 * peer[1] + peer[2]
            common = dict(send_sem=send_sems.at[k - 1], recv_sem=recv_sems.at[k - 1], device_id=peer,
                          device_id_type=pl.DeviceIdType.MESH)
            sends.append(pltpu.make_async_remote_copy(src_ref=src(peer_idx), dst_ref=out_ref.at[me], **common))
            recvs.append(pltpu.make_async_remote_copy(src_ref=src(peer_idx), dst_ref=out_ref.at[peer_idx], **common))
        for cp in sends:
            cp.start()
        for cp in recvs:
            cp.wait_recv()
        for cp in sends:
            cp.wait_send()
        local.wait()

    anyspace = pl.BlockSpec(memory_space=pl.ANY)
    return pl.pallas_call(
        body, name=name, in_specs=[anyspace], out_specs=anyspace,
        out_shape=jax.ShapeDtypeStruct((N_DEV,) + shape, p.dtype),
        scratch_shapes=[pltpu.SemaphoreType.DMA((N_DEV - 1,)), pltpu.SemaphoreType.DMA((N_DEV - 1,)),
                        pltpu.SemaphoreType.DMA],
    )(p)


def _adamw(parts, w, m, v):
    R, C = w.shape
    tr = _tile(R, 128, SUBLANE)
    c1 = 1.0 - ADAM_B1 ** ADAM_STEP
    c2 = 1.0 - ADAM_B2 ** ADAM_STEP

    def body(p_ref, w_ref, m_ref, v_ref, g_ref, d_ref, mo_ref, vo_ref):
        g = p_ref[0]
        for s in range(1, N_DEV):
            g = g + p_ref[s]
        mn = ADAM_B1 * m_ref[...] + (1.0 - ADAM_B1) * g
        vn = ADAM_B2 * v_ref[...] + (1.0 - ADAM_B2) * (g * g)
        m_hat = mn / c1
        v_hat = vn / c2
        g_ref[...] = g
        d_ref[...] = -ADAM_LR * (m_hat / (jnp.sqrt(v_hat) + ADAM_EPS) + ADAM_WD * w_ref[...])
        mo_ref[...] = mn
        vo_ref[...] = vn

    row = pl.BlockSpec((tr, C), lambda i: (i, 0))
    out = jax.ShapeDtypeStruct((R, C), F32)
    return _call(body, name="adamw", grid=(R // tr,),
                 in_specs=[pl.BlockSpec((N_DEV, tr, C), lambda i: (0, i, 0)), row, row, row],
                 out_specs=[row, row, row, row], out_shape=[out, out, out, out], sem=("parallel",))(parts, w, m, v)


_MATMUL_W = [("w_in_ab", 2), ("w_out_ab", 1), ("w_qkv_c", 2), ("w_out_c", 1), ("mem_wq", 1), ("mem_wk", 1),
             ("mem_wv", 1), ("mem_wo", 1), ("ffn_w1", 2), ("ffn_w3", 2), ("ffn_w2", 1)]
_SMALL_SHARDED = [("conv_w", 2), ("ln_g", 2), ("ln_b", 2)]
_REPLICATED = ["gmlp_ln_g", "gmlp_ln_b", "gmlp_w_s", "gmlp_b_s", "conv_b", "conv_gn_g", "conv_gn_b"]
_WEIGHTS = ["w_in_ab", "gmlp_ln_g", "gmlp_ln_b", "gmlp_w_s", "gmlp_b_s", "conv_w", "conv_b", "conv_gn_g",
            "conv_gn_b", "w_out_ab", "w_qkv_c", "w_out_c", "mem_wq", "mem_wk", "mem_wv", "mem_wo", "ffn_w1",
            "ffn_w3", "ffn_w2", "ln_g", "ln_b"]
_PACK_ORDER = [n for n, _ in _MATMUL_W] + [n for n, _ in _SMALL_SHARDED] + _REPLICATED
_PACK_ROW_MULT = 128


_FF_AXIS = {"ffn_w1": 2, "ffn_w3": 2, "ffn_w2": 1}


def _to_padded(name, a):
    axis = _FF_AXIS.get(name)
    if axis is None:
        return a
    widths = [(0, 0)] * a.ndim
    widths[axis] = (0, D_FF_SHARD_PAD - a.shape[axis])
    return jnp.pad(a, widths)


def _from_padded(name, a):
    axis = _FF_AXIS.get(name)
    return a if axis is None else lax.slice_in_dim(a, 0, D_FF_SHARD, axis=axis)


def _unshard(g, axis):
    full = jnp.moveaxis(g, 0, axis)
    shp = full.shape
    return full.reshape(shp[:axis] + (shp[axis] * shp[axis + 1],) + shp[axis + 2:])


def _split_shards(full, axis):
    shp = full.shape
    parts = full.reshape(shp[:axis] + (N_DEV, shp[axis] // N_DEV) + shp[axis + 1:])
    return jnp.moveaxis(parts, axis, 0)


def _pack_rows(flat_parts, lead=()):
    flat = jnp.concatenate(flat_parts, axis=-1)
    n = flat.shape[-1]
    per = PACK_COLS * _PACK_ROW_MULT
    total = -(-n // per) * per
    if total != n:
        flat = jnp.concatenate([flat, jnp.zeros(lead + (total - n,), flat.dtype)], axis=-1)
    return flat.reshape(lead + (total // PACK_COLS, PACK_COLS))


def _gather_weights(w):
    big = _pack_rows([w[n].astype(BF16).reshape(-1) for n, _ in _MATMUL_W])
    small = _pack_rows([w[n].reshape(-1) for n, _ in _SMALL_SHARDED])
    big_all = _exchange(big, True, "gather_matmul_weights").reshape(N_DEV, -1)
    small_all = _exchange(small, True, "gather_small_weights").reshape(N_DEV, -1)
    full = {n: w[n] for n in _REPLICATED}
    for table, src in ((_MATMUL_W, big_all), (_SMALL_SHARDED, small_all)):
        off = 0
        for n, axis in table:
            size = w[n].size
            full[n] = _unshard(src[:, off:off + size].reshape((N_DEV,) + w[n].shape), axis)
            off += size
    full["w_qkv_cT"] = jnp.swapaxes(full.pop("w_qkv_c"), 1, 2)
    full["ffn_w13"] = jnp.concatenate([full.pop("ffn_w1"), full.pop("ffn_w3")], axis=2)
    return full


def _pack_grads(grads):
    pieces = [_split_shards(grads[n], axis).reshape(N_DEV, -1) for n, axis in _MATMUL_W + _SMALL_SHARDED]
    pieces += [jnp.broadcast_to(grads[n].reshape(1, -1), (N_DEV, grads[n].size)) for n in _REPLICATED]
    return _pack_rows(pieces, lead=(N_DEV,))


def _pack_local(w):
    return _pack_rows([w[n].reshape(-1) for n in _PACK_ORDER])


def _unpack_local(packed, like):
    flat = packed.reshape(-1)
    out, off = {}, 0
    for n in _PACK_ORDER:
        out[n] = flat[off:off + like[n].size].reshape(like[n].shape)
        off += like[n].size
    return out


def kernel(x, mem, w_in_ab, gmlp_ln_g, gmlp_ln_b, gmlp_w_s, gmlp_b_s, conv_w, conv_b, conv_gn_g, conv_gn_b, w_out_ab, w_qkv_c, w_out_c, mem_wq, mem_wk, mem_wv, mem_wo, ffn_w1, ffn_w3, ffn_w2, ln_g, ln_b, loss_target, m_w_in_ab, m_gmlp_ln_g, m_gmlp_ln_b, m_gmlp_w_s, m_gmlp_b_s, m_conv_w, m_conv_b, m_conv_gn_g, m_conv_gn_b, m_w_out_ab, m_w_qkv_c, m_w_out_c, m_mem_wq, m_mem_wk, m_mem_wv, m_mem_wo, m_ffn_w1, m_ffn_w3, m_ffn_w2, m_ln_g, m_ln_b, v_w_in_ab, v_gmlp_ln_g, v_gmlp_ln_b, v_gmlp_w_s, v_gmlp_b_s, v_conv_w, v_conv_b, v_conv_gn_g, v_conv_gn_b, v_w_out_ab, v_w_qkv_c, v_w_out_c, v_mem_wq, v_mem_wk, v_mem_wv, v_mem_wo, v_ffn_w1, v_ffn_w3, v_ffn_w2, v_ln_g, v_ln_b):
    w = dict(w_in_ab=w_in_ab, gmlp_ln_g=gmlp_ln_g, gmlp_ln_b=gmlp_ln_b, gmlp_w_s=gmlp_w_s, gmlp_b_s=gmlp_b_s,
             conv_w=conv_w, conv_b=conv_b, conv_gn_g=conv_gn_g, conv_gn_b=conv_gn_b, w_out_ab=w_out_ab,
             w_qkv_c=w_qkv_c, w_out_c=w_out_c, mem_wq=mem_wq, mem_wk=mem_wk, mem_wv=mem_wv, mem_wo=mem_wo,
             ffn_w1=ffn_w1, ffn_w3=ffn_w3, ffn_w2=ffn_w2, ln_g=ln_g, ln_b=ln_b)
    m = dict(w_in_ab=m_w_in_ab, gmlp_ln_g=m_gmlp_ln_g, gmlp_ln_b=m_gmlp_ln_b, gmlp_w_s=m_gmlp_w_s,
             gmlp_b_s=m_gmlp_b_s, conv_w=m_conv_w, conv_b=m_conv_b, conv_gn_g=m_conv_gn_g,
             conv_gn_b=m_conv_gn_b, w_out_ab=m_w_out_ab, w_qkv_c=m_w_qkv_c, w_out_c=m_w_out_c, mem_wq=m_mem_wq,
             mem_wk=m_mem_wk, mem_wv=m_mem_wv, mem_wo=m_mem_wo, ffn_w1=m_ffn_w1, ffn_w3=m_ffn_w3,
             ffn_w2=m_ffn_w2, ln_g=m_ln_g, ln_b=m_ln_b)
    v = dict(w_in_ab=v_w_in_ab, gmlp_ln_g=v_gmlp_ln_g, gmlp_ln_b=v_gmlp_ln_b, gmlp_w_s=v_gmlp_w_s,
             gmlp_b_s=v_gmlp_b_s, conv_w=v_conv_w, conv_b=v_conv_b, conv_gn_g=v_conv_gn_g,
             conv_gn_b=v_conv_gn_b, w_out_ab=v_w_out_ab, w_qkv_c=v_w_qkv_c, w_out_c=v_w_out_c, mem_wq=v_mem_wq,
             mem_wk=v_mem_wk, mem_wv=v_mem_wv, mem_wo=v_mem_wo, ffn_w1=v_ffn_w1, ffn_w3=v_ffn_w3,
             ffn_w2=v_ffn_w2, ln_g=v_ln_g, ln_b=v_ln_b)

    w, m, v = ({n: _to_padded(n, t[n]) for n in _WEIGHTS} for t in (w, m, v))
    full = _gather_weights(w)
    loss_part, grad_x, grads = _local_step(x, mem, loss_target, full)
    loss = lax.psum(jnp.sum(loss_part), AXES)

    parts = _exchange(_pack_grads(grads), False, "exchange_grads")
    packed = _adamw(parts, _pack_local(w), _pack_local(m), _pack_local(v))
    g_o, d_o, m_o, v_o = ({n: _from_padded(n, a) for n, a in _unpack_local(t, w).items()} for t in packed)
    return (loss, grad_x, *[g_o[n] for n in _WEIGHTS], *[d_o[n] for n in _WEIGHTS],
            *[m_o[n] for n in _WEIGHTS], *[v_o[n] for n in _WEIGHTS])
```

```python
import functools

import jax
import jax.numpy as jnp
from jax import lax
from jax.experimental import pallas as pl
from jax.experimental.pallas import tpu as pltpu

F32 = jnp.float32
BF16 = jnp.bfloat16
_MXU = jnp.bfloat16

D_MODEL = 1024
DEPTH = 4
MEM_LEN = 256
CHUNK = 128
A_GROUPS = 4
A_WIDTH = 512
B_WIDTH = 512
GROUP_DIM = 128
CONV_WIDTH = 31
C_HEADS = 16
C_HEAD_DIM = 64
MEM_HEADS = 4
MEM_HEAD_DIM = 256
D_FF = 2816
ALPHA = (2.0 * DEPTH) ** 0.25
LN_EPS = 1e-5
SB_SCALE = C_HEAD_DIM ** -0.5
MEM_SCALE = MEM_HEAD_DIM ** -0.5

ADAM_LR = 0.001
ADAM_B1 = 0.9
ADAM_B2 = 0.999
ADAM_EPS = 1e-08
ADAM_WD = 0.01
ADAM_STEP = 10

N_DEV = 8
AXES = ("x", "y", "c")
LANE = 128
SUBLANE = 8
PACK_COLS = 1024
_VMEM_LIMIT = 56 * 1024 * 1024

SB_TQ = 256
SB_TK = 128
SB_HEADS = 2
SB_BWD_HEADS = 2
D_FF_SHARD = D_FF // 8
D_FF_SHARD_PAD = 384
CONV_ROWS = 256
CONV_PAD = 32


def _tile(n, cap, mult):
    best = None
    for d in range(mult, min(n, cap) + 1, mult):
        if n % d == 0:
            best = d
    return n if best is None else best


def _call(body, *, name, grid, in_specs, out_specs, out_shape, scratch=(), sem=None):
    return pl.pallas_call(
        body, name=name, grid=grid, in_specs=in_specs, out_specs=out_specs, out_shape=out_shape,
        scratch_shapes=list(scratch),
        compiler_params=pltpu.CompilerParams(dimension_semantics=sem, vmem_limit_bytes=_VMEM_LIMIT))


def _dg(a, b, ca, cb):
    return lax.dot_general(a.astype(_MXU), b.astype(_MXU), (((ca,), (cb,)), ((), ())),
                           preferred_element_type=F32)


@jax.custom_vjp
def _dot(a, b):
    return _dg(a, b, 1, 0)


_dot.defvjp(lambda a, b: (_dg(a, b, 1, 0), (a, b)),
            lambda r, g: (_dg(g, r[1], 1, 1), _dg(r[0], g, 0, 0)))


@jax.custom_vjp
def _dot_t(a, b):
    return _dg(a, b, 1, 1)


_dot_t.defvjp(lambda a, b: (_dg(a, b, 1, 1), (a, b)),
              lambda r, g: (_dg(g, r[1], 1, 0), _dg(g, r[0], 0, 0)))


def _norm(z, g, b):
    mu = jnp.mean(z, axis=-1, keepdims=True)
    zc = z - mu
    var = jnp.mean(zc * zc, axis=-1, keepdims=True)
    return zc * lax.rsqrt(var + LN_EPS) * g + b


def _gelu(x):
    return 0.5 * x * (1.0 + lax.erf(x * (0.5 ** 0.5)))


def _matmul(a, b, mode="nn", add=None, out_dtype=F32, la=None, lb=None, name="mm"):
    ash, bsh = a.shape[-2:], b.shape[-2:]
    if mode == "nn":
        (M, K), (K2, N) = ash, bsh
    elif mode == "nt":
        (M, K), (N, K2) = ash, bsh
    else:
        (K, M), (K2, N) = ash, bsh
    assert K == K2, (a.shape, b.shape, mode)
    tm = _tile(M, 512, LANE)
    tn = _tile(N, 1408, LANE)
    tk = _tile(K, 512 if mode == "tn" else 1408, LANE)
    nk = K // tk
    if mode == "tn":
        a_blk, a_idx = (tk, tm), (lambda i, j, k: (k, i))
    else:
        a_blk, a_idx = (tm, tk), (lambda i, j, k: (i, k))
    if mode == "nt":
        b_blk, b_idx = (tn, tk), (lambda i, j, k: (j, k))
    else:
        b_blk, b_idx = (tk, tn), (lambda i, j, k: (k, j))
    dims = {"nn": (1, 0), "nt": (1, 1), "tn": (0, 0)}[mode]

    def spec(blk, idx, lead):
        if lead is None:
            return pl.BlockSpec(blk, idx)
        return pl.BlockSpec((None,) + blk, lambda i, j, k: (lead,) + idx(i, j, k))

    has_add = add is not None

    def body(*refs):
        a_ref, b_ref = refs[0], refs[1]
        add_ref = refs[2] if has_add else None
        o_ref = refs[3] if has_add else refs[2]
        p = _dg(a_ref[...], b_ref[...], *dims)

        def finish(r):
            if has_add:
                r = r + add_ref[...].astype(F32)
            o_ref[...] = r.astype(out_dtype)

        if nk == 1:
            finish(p)
        else:
            acc = refs[-1]
            k = pl.program_id(2)

            @pl.when(k == 0)
            def _():
                acc[...] = p

            @pl.when(k > 0)
            def _():
                acc[...] += p

            @pl.when(k == nk - 1)
            def _():
                finish(acc[...])

    in_specs = [spec(a_blk, a_idx, la), spec(b_blk, b_idx, lb)]
    args = [a, b]
    if has_add:
        in_specs.append(pl.BlockSpec((tm, tn), lambda i, j, k: (i, j)))
        args.append(add)
    return _call(
        body, name=name, grid=(M // tm, N // tn, nk), in_specs=in_specs,
        out_specs=pl.BlockSpec((tm, tn), lambda i, j, k: (i, j)),
        out_shape=jax.ShapeDtypeStruct((M, N), out_dtype),
        scratch=[pltpu.VMEM((tm, tn), F32)] if nk > 1 else [],
        sem=("parallel", "parallel", "arbitrary"))(*args)


def _ln_fn(x, f, g, b):
    return _norm(ALPHA * x + f, g, b)


def _ln_fwd(x, f, g, b):
    T, D = x.shape
    tm = _tile(T, 256, SUBLANE)

    def body(x_ref, f_ref, g_ref, b_ref, o_ref):
        o_ref[...] = _ln_fn(x_ref[...], f_ref[...], g_ref[...], b_ref[...])

    row = pl.BlockSpec((tm, D), lambda i: (i, 0))
    par = pl.BlockSpec((1, D), lambda i: (0, 0))
    return _call(body, name="ln_fwd", grid=(T // tm,), in_specs=[row, row, par, par], out_specs=row,
                 out_shape=jax.ShapeDtypeStruct((T, D), F32), sem=("parallel",))(x, f, g, b)


def _ln_bwd(x, f, g, b, dy):
    T, D = x.shape
    tm = _tile(T, 256, SUBLANE)

    def body(x_ref, f_ref, g_ref, b_ref, dy_ref, dx_ref, df_ref, dg_ref, db_ref):
        _, vjp = jax.vjp(_ln_fn, x_ref[...], f_ref[...], g_ref[...], b_ref[...])
        dx, df, dg, db = vjp(dy_ref[...])
        dx_ref[...] = dx
        df_ref[...] = df.astype(BF16)

        @pl.when(pl.program_id(0) == 0)
        def _():
            dg_ref[...] = jnp.zeros_like(dg_ref)
            db_ref[...] = jnp.zeros_like(db_ref)

        dg_ref[...] += dg
        db_ref[...] += db

    row = pl.BlockSpec((tm, D), lambda i: (i, 0))
    par = pl.BlockSpec((1, D), lambda i: (0, 0))
    return _call(body, name="ln_bwd", grid=(T // tm,), in_specs=[row, row, par, par, row],
                 out_specs=[row, row, par, par],
                 out_shape=[jax.ShapeDtypeStruct((T, D), F32), jax.ShapeDtypeStruct((T, D), BF16),
                            jax.ShapeDtypeStruct((1, D), F32), jax.ShapeDtypeStruct((1, D), F32)],
                 sem=("arbitrary",))(x, f, g, b, dy)


def _loss_fwd(y, tgt):
    T, D = y.shape
    tm = _tile(T, 256, SUBLANE)

    def body(y_ref, t_ref, l_ref, dy_ref):
        d = y_ref[...] - t_ref[...]
        dy_ref[...] = d * (1.0 / D)

        @pl.when(pl.program_id(0) == 0)
        def _():
            l_ref[...] = jnp.zeros_like(l_ref)

        l_ref[...] += jnp.sum(d * d, axis=0, keepdims=True) * (0.5 / D)

    row = pl.BlockSpec((tm, D), lambda i: (i, 0))
    par = pl.BlockSpec((1, D), lambda i: (0, 0))
    return _call(body, name="loss", grid=(T // tm,), in_specs=[row, row], out_specs=[par, row],
                 out_shape=[jax.ShapeDtypeStruct((1, D), F32), jax.ShapeDtypeStruct((T, D), F32)],
                 sem=("arbitrary",))(y, tgt)


def _swiglu_fn(h1, h3):
    return h1 * jax.nn.sigmoid(h1) * h3


def _swiglu_fwd(h13):
    T, F2 = h13.shape
    F = F2 // 2
    tm = _tile(T, 256, SUBLANE)

    def body(h1_ref, h3_ref, o_ref):
        o_ref[...] = _swiglu_fn(h1_ref[...], h3_ref[...]).astype(BF16)

    return _call(body, name="swiglu_fwd", grid=(T // tm,),
                 in_specs=[pl.BlockSpec((tm, F), lambda i: (i, 0)), pl.BlockSpec((tm, F), lambda i: (i, 1))],
                 out_specs=pl.BlockSpec((tm, F), lambda i: (i, 0)),
                 out_shape=jax.ShapeDtypeStruct((T, F), BF16), sem=("parallel",))(h13, h13)


def _swiglu_bwd(h13, dact):
    T, F2 = h13.shape
    F = F2 // 2
    tm = _tile(T, 256, SUBLANE)

    def body(h1_ref, h3_ref, d_ref, o_ref):
        _, vjp = jax.vjp(_swiglu_fn, h1_ref[...], h3_ref[...])
        d1, d3 = vjp(d_ref[...])
        o_ref[:, :F] = d1.astype(BF16)
        o_ref[:, F:] = d3.astype(BF16)

    return _call(body, name="swiglu_bwd", grid=(T // tm,),
                 in_specs=[pl.BlockSpec((tm, F), lambda i: (i, 0)), pl.BlockSpec((tm, F), lambda i: (i, 1)),
                           pl.BlockSpec((tm, F), lambda i: (i, 0))],
                 out_specs=pl.BlockSpec((tm, F2), lambda i: (i, 0)),
                 out_shape=jax.ShapeDtypeStruct((T, F2), BF16), sem=("parallel",))(h13, h13, dact)


@jax.custom_vjp
def _chunkmix(wm, vn, bs):
    n = vn.shape[0] // CHUNK
    return jnp.concatenate([_dg(wm, vn[c * CHUNK:(c + 1) * CHUNK], 1, 0) + bs for c in range(n)], axis=0)


def _chunkmix_fwd(wm, vn, bs):
    return _chunkmix(wm, vn, bs), (wm, vn)


def _chunkmix_bwd(res, ct):
    wm, vn = res
    n = vn.shape[0] // CHUNK
    cts = [ct[c * CHUNK:(c + 1) * CHUNK] for c in range(n)]
    dvn = jnp.concatenate([_dg(wm, cts[c], 0, 0) for c in range(n)], axis=0)
    dwm = sum(_dg(cts[c], vn[c * CHUNK:(c + 1) * CHUNK], 1, 1) for c in range(n))
    dbs = sum(jnp.sum(cts[c], axis=1, keepdims=True) for c in range(n))
    return dwm, dvn, dbs


_chunkmix.defvjp(_chunkmix_fwd, _chunkmix_bwd)


def _gmlp_fn(u, v, lg, lb, ws, bs):
    ug = _gelu(u)
    vn = _norm(_gelu(v), lg, lb)
    r = lax.broadcasted_iota(jnp.int32, (CHUNK, CHUNK), 0)
    c = lax.broadcasted_iota(jnp.int32, (CHUNK, CHUNK), 1)
    wm = jnp.where(r >= c, ws, 0.0)
    return ug * _chunkmix(wm, vn, bs)


def _gmlp_specs(tm, order):
    def at(col0):
        return pl.BlockSpec((tm, GROUP_DIM), lambda *ids: (order(*ids)[0], col0 + order(*ids)[1]))
    par = pl.BlockSpec((1, GROUP_DIM), lambda *ids: (0, order(*ids)[1]))
    ws = pl.BlockSpec((None, CHUNK, CHUNK), lambda *ids: (order(*ids)[1], 0, 0))
    bs = pl.BlockSpec((None, CHUNK, 1), lambda *ids: (order(*ids)[1], 0, 0))
    return at, par, ws, bs


def _gmlp_fwd(hin, lg, lb, ws, bs):
    T = hin.shape[0]
    tm = _tile(T, 512, CHUNK)
    at, par, wsp, bsp = _gmlp_specs(tm, lambda i, g: (i, g))

    def body(u_ref, v_ref, lg_ref, lb_ref, ws_ref, bs_ref, o_ref):
        o_ref[...] = _gmlp_fn(u_ref[...], v_ref[...], lg_ref[...], lb_ref[...], ws_ref[...],
                              bs_ref[...]).astype(BF16)

    return _call(body, name="gmlp_fwd", grid=(T // tm, A_GROUPS),
                 in_specs=[at(0), at(A_GROUPS), par, par, wsp, bsp], out_specs=at(0),
                 out_shape=jax.ShapeDtypeStruct((T, A_WIDTH), BF16),
                 sem=("parallel", "parallel"))(hin, hin, lg, lb, ws, bs)


def _gmlp_bwd(hin, lg, lb, ws, bs, dyab):
    T = hin.shape[0]
    tm = _tile(T, 512, CHUNK)
    at, par, wsp, bsp = _gmlp_specs(tm, lambda g, i: (i, g))

    def body(u_ref, v_ref, lg_ref, lb_ref, ws_ref, bs_ref, dy_ref,
             du_ref, dv_ref, dlg_ref, dlb_ref, dws_ref, dbs_ref):
        _, vjp = jax.vjp(_gmlp_fn, u_ref[...], v_ref[...], lg_ref[...], lb_ref[...], ws_ref[...], bs_ref[...])
        du, dv, dlg, dlb, dws, dbs = vjp(dy_ref[...])
        du_ref[...] = du.astype(BF16)
        dv_ref[...] = dv.astype(BF16)

        @pl.when(pl.program_id(1) == 0)
        def _():
            dlg_ref[...] = jnp.zeros_like(dlg_ref)
            dlb_ref[...] = jnp.zeros_like(dlb_ref)
            dws_ref[...] = jnp.zeros_like(dws_ref)
            dbs_ref[...] = jnp.zeros_like(dbs_ref)

        dlg_ref[...] += dlg
        dlb_ref[...] += dlb
        dws_ref[...] += dws
        dbs_ref[...] += dbs

    half = jax.ShapeDtypeStruct((T, A_WIDTH), BF16)
    return _call(body, name="gmlp_bwd", grid=(A_GROUPS, T // tm),
                 in_specs=[at(0), at(A_GROUPS), par, par, wsp, bsp, at(0)],
                 out_specs=[at(0), at(0), par, par, wsp, bsp],
                 out_shape=[half, half, jax.ShapeDtypeStruct((1, A_WIDTH), F32),
                            jax.ShapeDtypeStruct((1, A_WIDTH), F32),
                            jax.ShapeDtypeStruct((A_GROUPS, CHUNK, CHUNK), F32),
                            jax.ShapeDtypeStruct((A_GROUPS, CHUNK, 1), F32)],
                 sem=("parallel", "arbitrary"))(hin, hin, lg, lb, ws, bs, dyab)


def _glu_fn(a, gt):
    return a * jax.nn.sigmoid(gt)


def _gn_silu_fn(c, cb, gg, gb):
    y = _norm(c + cb, gg, gb)
    return y * jax.nn.sigmoid(y)


def _conv_taps(w_ref, src_ref, row0, first):
    acc = None
    for k in range(CONV_WIDTH):
        term = w_ref[pl.ds(k, 1), :] * src_ref[pl.ds(row0 + first(k), CONV_ROWS), :]
        acc = term if acc is None else acc + term
    return acc


def _conv_specs(S, order):
    def at(col0):
        return pl.BlockSpec((S, GROUP_DIM), lambda *ids: (order(*ids)[0], col0 + order(*ids)[1]))
    par = pl.BlockSpec((1, GROUP_DIM), lambda *ids: (0, order(*ids)[1]))
    cw = pl.BlockSpec((CONV_WIDTH, GROUP_DIM), lambda *ids: (0, order(*ids)[1]))
    return at, par, cw


def _conv_fwd(hin, cw, cb, gg, gb, n_ex, S):
    T = hin.shape[0]
    at, par, cwp = _conv_specs(S, lambda e, g: (e, g))
    ngrp = B_WIDTH // GROUP_DIM
    lead = CONV_PAD - (CONV_WIDTH - 1)

    def body(a_ref, gt_ref, cw_ref, cb_ref, gg_ref, gb_ref, o_ref, pad_ref):
        pad_ref[0:CONV_PAD, :] = jnp.zeros((CONV_PAD, GROUP_DIM), F32)
        for r in range(S // CONV_ROWS):
            rows = pl.ds(r * CONV_ROWS, CONV_ROWS)
            pad_ref[pl.ds(CONV_PAD + r * CONV_ROWS, CONV_ROWS), :] = _glu_fn(a_ref[rows, :], gt_ref[rows, :])
        for r in range(S // CONV_ROWS):
            c = _conv_taps(cw_ref, pad_ref, r * CONV_ROWS, lambda k: lead + k)
            o_ref[pl.ds(r * CONV_ROWS, CONV_ROWS), :] = _gn_silu_fn(
                c, cb_ref[...], gg_ref[...], gb_ref[...]).astype(BF16)

    return _call(body, name="conv_fwd", grid=(n_ex, ngrp),
                 in_specs=[at(2 * A_GROUPS), at(2 * A_GROUPS + ngrp), cwp, par, par, par], out_specs=at(0),
                 out_shape=jax.ShapeDtypeStruct((T, B_WIDTH), BF16),
                 scratch=[pltpu.VMEM((S + CONV_PAD, GROUP_DIM), F32)],
                 sem=("parallel", "parallel"))(hin, hin, cw, cb, gg, gb)


def _conv_bwd(hin, cw, cb, gg, gb, dyab, n_ex, S):
    T = hin.shape[0]
    at, par, cwp = _conv_specs(S, lambda g, e: (e, g))
    ngrp = B_WIDTH // GROUP_DIM
    lead = CONV_PAD - (CONV_WIDTH - 1)
    nchunk = S // CONV_ROWS

    def body(a_ref, gt_ref, cw_ref, cb_ref, gg_ref, gb_ref, dy_ref,
             da_ref, dgt_ref, dcw_ref, dcb_ref, dgg_ref, dgb_ref, pad_ref, dcp_ref, wacc_ref, pacc_ref):
        @pl.when(pl.program_id(1) == 0)
        def _():
            wacc_ref[...] = jnp.zeros_like(wacc_ref)
            pacc_ref[...] = jnp.zeros_like(pacc_ref)

        pad_ref[0:CONV_PAD, :] = jnp.zeros((CONV_PAD, GROUP_DIM), F32)
        dcp_ref[S:S + CONV_PAD, :] = jnp.zeros((CONV_PAD, GROUP_DIM), F32)
        for r in range(nchunk):
            rows = pl.ds(r * CONV_ROWS, CONV_ROWS)
            pad_ref[pl.ds(CONV_PAD + r * CONV_ROWS, CONV_ROWS), :] = _glu_fn(a_ref[rows, :], gt_ref[rows, :])
        for r in range(nchunk):
            rows = pl.ds(r * CONV_ROWS, CONV_ROWS)
            c = _conv_taps(cw_ref, pad_ref, r * CONV_ROWS, lambda k: lead + k)
            _, vjp = jax.vjp(_gn_silu_fn, c, cb_ref[...], gg_ref[...], gb_ref[...])
            dc, dcb, dgg, dgb = vjp(dy_ref[rows, :])
            dcp_ref[rows, :] = dc
            pacc_ref[0:1, :] += dcb
            pacc_ref[1:2, :] += dgg
            pacc_ref[2:3, :] += dgb
        for r in range(nchunk):
            rows = pl.ds(r * CONV_ROWS, CONV_ROWS)
            dh = _conv_taps(cw_ref, dcp_ref, r * CONV_ROWS, lambda k: CONV_WIDTH - 1 - k)
            _, vjp = jax.vjp(_glu_fn, a_ref[rows, :], gt_ref[rows, :])
            da, dgt = vjp(dh)
            da_ref[rows, :] = da.astype(BF16)
            dgt_ref[rows, :] = dgt.astype(BF16)
            dc = dcp_ref[rows, :]
            for k in range(CONV_WIDTH):
                prod = dc * pad_ref[pl.ds(r * CONV_ROWS + lead + k, CONV_ROWS), :]
                wacc_ref[k] += jnp.sum(prod.reshape(CONV_ROWS // SUBLANE, SUBLANE, GROUP_DIM), axis=0)
        for k in range(CONV_WIDTH):
            dcw_ref[pl.ds(k, 1), :] = jnp.sum(wacc_ref[k], axis=0, keepdims=True)
        dcb_ref[...] = pacc_ref[0:1, :]
        dgg_ref[...] = pacc_ref[1:2, :]
        dgb_ref[...] = pacc_ref[2:3, :]

    half = jax.ShapeDtypeStruct((T, B_WIDTH), BF16)
    vec = jax.ShapeDtypeStruct((1, B_WIDTH), F32)
    return _call(body, name="conv_bwd", grid=(ngrp, n_ex),
                 in_specs=[at(2 * A_GROUPS), at(2 * A_GROUPS + ngrp), cwp, par, par, par, at(ngrp)],
                 out_specs=[at(0), at(0), cwp, par, par, par],
                 out_shape=[half, half, jax.ShapeDtypeStruct((CONV_WIDTH, B_WIDTH), F32), vec, vec, vec],
                 scratch=[pltpu.VMEM((S + CONV_PAD, GROUP_DIM), F32), pltpu.VMEM((S + CONV_PAD, GROUP_DIM), F32),
                          pltpu.VMEM((CONV_WIDTH, SUBLANE, GROUP_DIM), F32), pltpu.VMEM((SUBLANE, GROUP_DIM), F32)],
                 sem=("parallel", "arbitrary"))(hin, hin, cw, cb, gg, gb, dyab)


def _sb_consts():
    r = lax.broadcasted_iota(jnp.int32, (SB_TK + SUBLANE, 2 * SB_TK), 0)
    c = lax.broadcasted_iota(jnp.int32, (SB_TK + SUBLANE, 2 * SB_TK), 1) % SB_TK
    tail = r >= SB_TK
    u_after = jnp.where((c > r) | tail, 1.0, 0.0).astype(BF16)
    u_before = jnp.where((c < r) | tail, 1.0, 0.0).astype(BF16)
    s = lax.broadcasted_iota(jnp.int32, (SB_TK, SB_TQ), 0)
    t = lax.broadcasted_iota(jnp.int32, (SB_TK, SB_TQ), 1)
    masks = [(s + SB_TK * d) < t for d in range(SB_TQ // SB_TK)]
    return u_after, u_before, masks


def _sb_store_split(hl, h, j, x):
    hi = x.astype(BF16)
    hl[h, j, :SB_TK] = hi
    hl[h, j, SB_TK:] = (x - hi.astype(F32)).astype(BF16)


def _split_sum(u, x):
    hi = x.astype(BF16)
    lo = (x - hi.astype(F32)).astype(BF16)
    res = _dg(u, jnp.concatenate([hi, lo], axis=0), 1, 0)
    return res[:SB_TK], res[SB_TK:]


def _sb_sums(u, hl, ws, tot, h, j):
    res = _dg(u, hl[h, j], 1, 0)
    ws[h, j] = res[:SB_TK]
    tot[h, j] = res[SB_TK:]


def _sb_tiles(trips, per, fn, carry=None, descending=False):
    def trip(t, c):
        t = trips - 1 - t if descending else t
        for u in (reversed(range(per)) if descending else range(per)):
            if carry is None:
                fn(per * t + u)
            else:
                c = fn(per * t + u, c)
        return c

    return lax.fori_loop(0, trips, trip, 0 if carry is None else carry)


def _add8(x, row8):
    return (x.reshape(-1, SUBLANE, x.shape[-1]) + row8[None]).reshape(x.shape)


def _sb_blocked(dst, src, h, n, width):
    for t in range(n):
        dst[h, t] = src[h, :, t * width:(t + 1) * width].astype(dst.dtype)


def _sb_spec(S, nh):
    def at(part):
        return pl.BlockSpec((nh, C_HEAD_DIM, S), lambda e, h: (part * (C_HEADS // nh) + h, 0, e))
    return at


def _sb_fwd(qkvT, n_ex, S):
    T = qkvT.shape[-1]
    nq, nk = S // SB_TQ, S // SB_TK
    per = SB_TQ // SB_TK
    heads = range(SB_HEADS)

    def body(q_ref, k_ref, v_ref, o_ref, qb, kb, vb, ob, zs, hl, ws, tot, ab):
        for h in heads:
            _sb_blocked(qb, q_ref, h, nq, SB_TQ)
            _sb_blocked(kb, k_ref, h, nk, SB_TK)
            _sb_blocked(vb, v_ref, h, nk, SB_TK)
        u_after, _, masks = _sb_consts()

        def soft(h, j, mask):
            z = zs[h, j]
            sp = jnp.maximum(z, 0.0) + jnp.log1p(jnp.exp(-jnp.abs(z)))
            _sb_store_split(hl, h, j, -sp if mask is None else jnp.where(mask, -sp, 0.0))
            zs[h, j] = z - sp

        def weigh(h, j, carry, mask):
            att = jnp.exp(zs[h, j] + _add8(ws[h, j], carry))
            if mask is not None:
                att = jnp.where(mask, att, 0.0)
            ab[h, j] = att.astype(_MXU)
            return carry + tot[h, j]

        def qtile(i, _):
            qs = [qb[h, i] for h in heads]

            def scores(j):
                for h in heads:
                    zs[h, j] = _dg(kb[h, j], qs[h], 0, 0) * SB_SCALE

            _sb_tiles(i + 1, per, scores)
            _sb_tiles(i, per, lambda j: [soft(h, j, None) for h in heads])
            for d in range(per):
                for h in heads:
                    soft(h, per * i + d, masks[d])
            _sb_tiles(i + 1, per, lambda j: [_sb_sums(u_after, hl, ws, tot, h, j) for h in heads])
            carry = tuple(jnp.zeros((SUBLANE, SB_TQ), F32) for h in heads)
            for d in reversed(range(per)):
                carry = tuple(weigh(h, per * i + d, carry[h], masks[d]) for h in heads)
            _sb_tiles(i, per, lambda j, c: tuple(weigh(h, j, c[h], None) for h in heads), carry, descending=True)
            acc = tuple(jnp.zeros((C_HEAD_DIM, SB_TQ), F32) for h in heads)
            acc = _sb_tiles(i + 1, per, lambda j, a: tuple(a[h] + _dg(vb[h, j], ab[h, j], 1, 0) for h in heads), acc)
            for h in heads:
                ob[h, i] = acc[h]
            return 0

        lax.fori_loop(0, nq, qtile, 0)
        for h in heads:
            for i in range(nq):
                o_ref[h, :, i * SB_TQ:(i + 1) * SB_TQ] = ob[h, i].astype(BF16)

    at = _sb_spec(S, SB_HEADS)
    qshape, kshape = (SB_HEADS, nq, C_HEAD_DIM, SB_TQ), (SB_HEADS, nk, C_HEAD_DIM, SB_TK)
    tiles = (SB_HEADS, nk, SB_TK, SB_TQ)
    return _call(body, name="sb_fwd", grid=(n_ex, C_HEADS // SB_HEADS), in_specs=[at(0), at(1), at(2)],
                 out_specs=at(0), out_shape=jax.ShapeDtypeStruct((C_HEADS, C_HEAD_DIM, T), BF16),
                 scratch=[pltpu.VMEM(qshape, _MXU), pltpu.VMEM(kshape, _MXU), pltpu.VMEM(kshape, _MXU),
                          pltpu.VMEM(qshape, F32), pltpu.VMEM(tiles, F32),
                          pltpu.VMEM((SB_HEADS, nk, 2 * SB_TK, SB_TQ), BF16), pltpu.VMEM(tiles, F32),
                          pltpu.VMEM((SB_HEADS, nk, SUBLANE, SB_TQ), F32), pltpu.VMEM(tiles, _MXU)],
                 sem=("parallel", "parallel"))(qkvT, qkvT, qkvT)


def _sb_bwd(qkvT, doT, n_ex, S):
    T = qkvT.shape[-1]
    nq, nk = S // SB_TQ, S // SB_TK
    per = SB_TQ // SB_TK
    heads = range(SB_BWD_HEADS)

    def body(q_ref, k_ref, v_ref, do_ref, dq_ref, dk_ref, dv_ref, qb, kb, vb, dob, dqa, dka, dva, dl_s, sg_s):
        for h in heads:
            _sb_blocked(qb, q_ref, h, nq, SB_TQ)
            _sb_blocked(dob, do_ref, h, nq, SB_TQ)
            _sb_blocked(kb, k_ref, h, nk, SB_TK)
            _sb_blocked(vb, v_ref, h, nk, SB_TK)
        dka[...] = jnp.zeros_like(dka)
        dva[...] = jnp.zeros_like(dva)
        u_after, u_before, masks = _sb_consts()

        def rebuild(h, j, qi, gi, carry, mask):
            z = _dg(kb[h, j], qi, 0, 0) * SB_SCALE
            e = jnp.exp(-jnp.abs(z))
            sp = jnp.maximum(z, 0.0) + jnp.log1p(e)
            within, total = _split_sum(u_after, -sp if mask is None else jnp.where(mask, -sp, 0.0))
            att = jnp.exp((z - sp) + _add8(within, carry))
            r = 1.0 / (1.0 + e)
            sig = jnp.where(z >= 0.0, r, e * r)
            if mask is not None:
                att = jnp.where(mask, att, 0.0)
                sig = jnp.where(mask, sig, 0.0)
            dl_s[h, j] = _dg(vb[h, j], gi, 0, 0) * att
            sg_s[h, j] = sig
            dva[h, j] += _dg(gi, att, 1, 1)
            return carry + total

        def push(h, j, qi, carry, dq):
            dlogit, sig = dl_s[h, j], sg_s[h, j]
            within, total = _split_sum(u_before, dlogit)
            dz = (dlogit * (1.0 - sig) - sig * _add8(within, carry)) * SB_SCALE
            dka[h, j] += _dg(qi, dz, 1, 1)
            return carry + total, dq + _dg(kb[h, j], dz, 1, 0)

        def qtile(i, _):
            qs = [qb[h, i] for h in heads]
            gs = [dob[h, i] for h in heads]
            carry = tuple(jnp.zeros((SUBLANE, SB_TQ), F32) for h in heads)
            for d in reversed(range(per)):
                carry = tuple(rebuild(h, per * i + d, qs[h], gs[h], carry[h], masks[d]) for h in heads)
            _sb_tiles(i, per, lambda j, c: tuple(rebuild(h, j, qs[h], gs[h], c[h], None) for h in heads), carry,
                      descending=True)
            st = tuple((jnp.zeros((SUBLANE, SB_TQ), F32), jnp.zeros((C_HEAD_DIM, SB_TQ), F32)) for h in heads)
            st = _sb_tiles(i + 1, per, lambda j, s: tuple(push(h, j, qs[h], *s[h]) for h in heads), st)
            for h in heads:
                dqa[h, i] = st[h][1]
            return 0

        lax.fori_loop(0, nq, qtile, 0)
        for h in heads:
            for i in range(nq):
                dq_ref[h, :, i * SB_TQ:(i + 1) * SB_TQ] = dqa[h, i].astype(BF16)
            for j in range(nk):
                dk_ref[h, :, j * SB_TK:(j + 1) * SB_TK] = dka[h, j].astype(BF16)
                dv_ref[h, :, j * SB_TK:(j + 1) * SB_TK] = dva[h, j].astype(BF16)

    nh = SB_BWD_HEADS
    at = _sb_spec(S, nh)
    out = jax.ShapeDtypeStruct((C_HEADS, C_HEAD_DIM, T), BF16)
    qshape, kshape = (nh, nq, C_HEAD_DIM, SB_TQ), (nh, nk, C_HEAD_DIM, SB_TK)
    tiles = (nh, nk, SB_TK, SB_TQ)
    return _call(body, name="sb_bwd", grid=(n_ex, C_HEADS // nh), in_specs=[at(0), at(1), at(2), at(0)],
                 out_specs=[at(0), at(0), at(0)], out_shape=[out, out, out],
                 scratch=[pltpu.VMEM(qshape, _MXU), pltpu.VMEM(kshape, _MXU), pltpu.VMEM(kshape, _MXU),
                          pltpu.VMEM(qshape, _MXU), pltpu.VMEM(qshape, F32), pltpu.VMEM(kshape, F32),
                          pltpu.VMEM(kshape, F32), pltpu.VMEM(tiles, F32), pltpu.VMEM(tiles, F32)],
                 sem=("parallel", "parallel"))(qkvT, qkvT, qkvT, doT)


def _xattn_fn(q, k, v):
    s = _dot_t(q, k) * MEM_SCALE
    m = lax.stop_gradient(jnp.max(s, axis=-1, keepdims=True))
    p = jnp.exp(s - m)
    p = p / jnp.sum(p, axis=-1, keepdims=True)
    return _dot(p, v)


def _xattn_fwd(q, kk, vv, n_ex, S):
    T = q.shape[0]
    tq = _tile(S, 1024, SUBLANE)
    nqt = S // tq

    def body(q_ref, k_ref, v_ref, o_ref):
        o_ref[...] = _xattn_fn(q_ref[...], k_ref[...], v_ref[...]).astype(BF16)

    qs = pl.BlockSpec((tq, MEM_HEAD_DIM), lambda e, i, h: (e * nqt + i, h))
    ks = pl.BlockSpec((MEM_LEN, MEM_HEAD_DIM), lambda e, i, h: (e, h))
    return _call(body, name="xattn_fwd", grid=(n_ex, nqt, MEM_HEADS), in_specs=[qs, ks, ks], out_specs=qs,
                 out_shape=jax.ShapeDtypeStruct((T, D_MODEL), BF16),
                 sem=("parallel", "parallel", "parallel"))(q, kk, vv)


def _xattn_bwd(q, kk, vv, do, n_ex, S):
    T = q.shape[0]
    tq = _tile(S, 1024, SUBLANE)
    nqt = S // tq

    def body(q_ref, k_ref, v_ref, do_ref, dq_ref, dk_ref, dv_ref):
        _, vjp = jax.vjp(_xattn_fn, q_ref[...].astype(F32), k_ref[...].astype(F32), v_ref[...].astype(F32))
        dq, dk, dv = vjp(do_ref[...].astype(F32))
        dq_ref[...] = dq.astype(BF16)

        @pl.when(pl.program_id(2) == 0)
        def _():
            dk_ref[...] = jnp.zeros_like(dk_ref)
            dv_ref[...] = jnp.zeros_like(dv_ref)

        dk_ref[...] += dk
        dv_ref[...] += dv

    qs = pl.BlockSpec((tq, MEM_HEAD_DIM), lambda e, h, i: (e * nqt + i, h))
    ks = pl.BlockSpec((MEM_LEN, MEM_HEAD_DIM), lambda e, h, i: (e, h))
    kv = jax.ShapeDtypeStruct(kk.shape, F32)
    return _call(body, name="xattn_bwd", grid=(n_ex, MEM_HEADS, nqt), in_specs=[qs, ks, ks, qs],
                 out_specs=[qs, ks, ks], out_shape=[jax.ShapeDtypeStruct((T, D_MODEL), BF16), kv, kv],
                 sem=("parallel", "parallel", "arbitrary"))(q, kk, vv, do)


def _local_step(x, mem, tgt, W):
    n_ex, S, D = x.shape
    T = n_ex * S
    h = x.reshape(T, D)
    mem2 = mem.reshape(n_ex * MEM_LEN, D)
    row = lambda v: v.reshape(1, -1)
    saved = []
    for l in range(DEPTH):
        sv = dict(x0=h)
        if l % 2 == 0:
            e = l // 2
            hin = _matmul(h, W["w_in_ab"], lb=e, name="in_proj")
            ya = _gmlp_fwd(hin, row(W["gmlp_ln_g"][e]), row(W["gmlp_ln_b"][e]), W["gmlp_w_s"][e],
                           W["gmlp_b_s"][e][:, :, None])
            yb = _conv_fwd(hin, W["conv_w"][e], row(W["conv_b"][e]), row(W["conv_gn_g"][e]),
                           row(W["conv_gn_b"][e]), n_ex, S)
            yab = jnp.concatenate([ya, yb], axis=1)
            mix = _matmul(yab, W["w_out_ab"], lb=e, name="out_proj")
            sv.update(hin=hin, yab=yab)
        else:
            o = l // 2
            qkvT = _matmul(W["w_qkv_cT"], h, mode="nt", la=o, out_dtype=BF16, name="qkv_proj")
            yT = _sb_fwd(qkvT.reshape(3 * C_HEADS, C_HEAD_DIM, T), n_ex, S)
            mix = _matmul(yT.reshape(D, T), W["w_out_c"], mode="tn", lb=o, name="sb_out_proj")
            sv.update(qkvT=qkvT, yT=yT)
        x1 = _ln_fwd(h, mix, row(W["ln_g"][l, 0]), row(W["ln_b"][l, 0]))
        q = _matmul(x1, W["mem_wq"], lb=l, out_dtype=BF16, name="mem_q")
        kk = _matmul(mem2, W["mem_wk"], lb=l, out_dtype=BF16, name="mem_kv")
        vv = _matmul(mem2, W["mem_wv"], lb=l, out_dtype=BF16, name="mem_kv")
        oc = _xattn_fwd(q, kk, vv, n_ex, S)
        cross = _matmul(oc, W["mem_wo"], lb=l, name="mem_o")
        x2 = _ln_fwd(x1, cross, row(W["ln_g"][l, 1]), row(W["ln_b"][l, 1]))
        h13 = _matmul(x2, W["ffn_w13"], lb=l, name="ffn_in")
        act = _swiglu_fwd(h13)
        f = _matmul(act, W["ffn_w2"], lb=l, name="ffn_out")
        x3 = _ln_fwd(x2, f, row(W["ln_g"][l, 2]), row(W["ln_b"][l, 2]))
        sv.update(mix=mix, x1=x1, q=q, kk=kk, vv=vv, oc=oc, cross=cross, x2=x2, h13=h13, act=act, f=f)
        saved.append(sv)
        h = x3

    loss_part, dh = _loss_fwd(h, tgt.reshape(T, D))

    G = {k: [None] * n for k, n in [
        ("w_in_ab", 2), ("gmlp_ln_g", 2), ("gmlp_ln_b", 2), ("gmlp_w_s", 2), ("gmlp_b_s", 2), ("conv_w", 2),
        ("conv_b", 2), ("conv_gn_g", 2), ("conv_gn_b", 2), ("w_out_ab", 2), ("w_qkv_c", 2), ("w_out_c", 2),
        ("mem_wq", 4), ("mem_wk", 4), ("mem_wv", 4), ("mem_wo", 4), ("ffn_w1", 4), ("ffn_w3", 4),
        ("ffn_w2", 4), ("ln_g", 4), ("ln_b", 4)]}
    for l in reversed(range(DEPTH)):
        sv = saved[l]
        lng, lnb = [None] * 3, [None] * 3
        dx2, df, lng[2], lnb[2] = _ln_bwd(sv["x2"], sv["f"], row(W["ln_g"][l, 2]), row(W["ln_b"][l, 2]), dh)
        dact = _matmul(df, W["ffn_w2"], mode="nt", lb=l, name="ffn_out_dx")
        G["ffn_w2"][l] = _matmul(sv["act"], df, mode="tn", name="ffn_out_dw")
        dh13 = _swiglu_bwd(sv["h13"], dact)
        dx2 = _matmul(dh13, W["ffn_w13"], mode="nt", lb=l, add=dx2, name="ffn_in_dx")
        dw13 = _matmul(sv["x2"], dh13, mode="tn", name="ffn_in_dw")
        ff = dw13.shape[1] // 2
        G["ffn_w1"][l], G["ffn_w3"][l] = dw13[:, :ff], dw13[:, ff:]
        dx1, dcross, lng[1], lnb[1] = _ln_bwd(sv["x1"], sv["cross"], row(W["ln_g"][l, 1]),
                                              row(W["ln_b"][l, 1]), dx2)
        doc = _matmul(dcross, W["mem_wo"], mode="nt", lb=l, out_dtype=BF16, name="mem_o_dx")
        G["mem_wo"][l] = _matmul(sv["oc"], dcross, mode="tn", name="mem_o_dw")
        dq, dkk, dvv = _xattn_bwd(sv["q"], sv["kk"], sv["vv"], doc, n_ex, S)
        dx1 = _matmul(dq, W["mem_wq"], mode="nt", lb=l, add=dx1, name="mem_q_dx")
        G["mem_wq"][l] = _matmul(sv["x1"], dq, mode="tn", name="mem_q_dw")
        G["mem_wk"][l] = _matmul(mem2, dkk, mode="tn", name="mem_kv_dw")
        G["mem_wv"][l] = _matmul(mem2, dvv, mode="tn", name="mem_kv_dw")
        dx0, dmix, lng[0], lnb[0] = _ln_bwd(sv["x0"], sv["mix"], row(W["ln_g"][l, 0]), row(W["ln_b"][l, 0]), dx1)
        if l % 2 == 0:
            e = l // 2
            dyab = _matmul(dmix, W["w_out_ab"], mode="nt", lb=e, name="out_proj_dx")
            G["w_out_ab"][e] = _matmul(sv["yab"], dmix, mode="tn", name="out_proj_dw")
            du, dv, dlg, dlb, dws, dbs = _gmlp_bwd(
                sv["hin"], row(W["gmlp_ln_g"][e]), row(W["gmlp_ln_b"][e]), W["gmlp_w_s"][e],
                W["gmlp_b_s"][e][:, :, None], dyab)
            da, dgt, dcw, dcb, dgg, dgb = _conv_bwd(
                sv["hin"], W["conv_w"][e], row(W["conv_b"][e]), row(W["conv_gn_g"][e]),
                row(W["conv_gn_b"][e]), dyab, n_ex, S)
            dhin = jnp.concatenate([du, dv, da, dgt], axis=1)
            dh = _matmul(dhin, W["w_in_ab"], mode="nt", lb=e, add=dx0, name="in_proj_dx")
            G["w_in_ab"][e] = _matmul(sv["x0"], dhin, mode="tn", name="in_proj_dw")
            G["gmlp_ln_g"][e], G["gmlp_ln_b"][e] = dlg[0], dlb[0]
            G["gmlp_w_s"][e], G["gmlp_b_s"][e] = dws, dbs[:, :, 0]
            G["conv_w"][e], G["conv_b"][e], G["conv_gn_g"][e], G["conv_gn_b"][e] = dcw, dcb[0], dgg[0], dgb[0]
        else:
            o = l // 2
            dyT = _matmul(W["w_out_c"], dmix, mode="nt", la=o, out_dtype=BF16, name="sb_out_proj_dx")
            G["w_out_c"][o] = _matmul(sv["yT"].reshape(D, T), dmix, name="sb_out_proj_dw")
            dqT, dkT, dvT = _sb_bwd(sv["qkvT"].reshape(3 * C_HEADS, C_HEAD_DIM, T),
                                    dyT.reshape(C_HEADS, C_HEAD_DIM, T), n_ex, S)
            dqkvT = jnp.concatenate([dqT, dkT, dvT], axis=0).reshape(3 * D, T)
            dh = _matmul(dqkvT, W["w_qkv_cT"], mode="tn", lb=o, add=dx0, name="qkv_proj_dx")
            G["w_qkv_c"][o] = _matmul(dqkvT, sv["x0"], name="qkv_proj_dw").T
        G["ln_g"][l] = jnp.concatenate(lng, axis=0)
        G["ln_b"][l] = jnp.concatenate(lnb, axis=0)
    grads = {k: jnp.stack(v, axis=0) for k, v in G.items()}
    return loss_part, dh.reshape(n_ex, S, D), grads


def _allgather(p, name):
    def body(p_ref, out_ref, send_sems, recv_sems, local_sem):
        x, y, c = (lax.axis_index(a) for a in AXES)
        me, sibling = (x, y, c), (x, y, 1 - c)
        chips = [(1 - x, y), (x, 1 - y), (1 - x, 1 - y)]

        def rows(px, py, pc):
            return out_ref.at[4 * px + 2 * py + pc]

        def copy(k, block, to, src=None):
            return pltpu.make_async_remote_copy(
                src_ref=rows(*block) if src is None else src, dst_ref=rows(*block), send_sem=send_sems.at[k],
                recv_sem=recv_sems.at[k], device_id=to, device_id_type=pl.DeviceIdType.MESH)

        mine = pltpu.make_async_copy(p_ref, rows(*me), local_sem)
        mine.start()
        first = [copy(0, me, sibling, src=p_ref)]
        first += [copy(1 + j, me, (*chip, c), src=p_ref) for j, chip in enumerate(chips)]
        for cp in first:
            cp.start()
        passed = [copy(4 + j, (*chip, c), sibling) for j, chip in enumerate(chips)]
        for j, chip in enumerate(chips):
            copy(1 + j, (*chip, c), me).wait_recv()
            passed[j].start()
        copy(0, sibling, me).wait_recv()
        for j, chip in enumerate(chips):
            copy(4 + j, (*chip, 1 - c), me).wait_recv()
        for cp in first + passed:
            cp.wait_send()
        mine.wait()

    anyspace = pl.BlockSpec(memory_space=pl.ANY)
    return pl.pallas_call(
        body, name=name, in_specs=[anyspace], out_specs=anyspace,
        out_shape=jax.ShapeDtypeStruct((N_DEV,) + p.shape, p.dtype),
        scratch_shapes=[pltpu.SemaphoreType.DMA((N_DEV - 1,)), pltpu.SemaphoreType.DMA((N_DEV - 1,)),
                        pltpu.SemaphoreType.DMA],
    )(p)


def _exchange(p, name):
    shape = p.shape[-2:]

    def body(p_ref, out_ref, send_sems, recv_sems, local_sem):
        pos = [lax.axis_index(a) for a in AXES]
        me = 4 * pos[0] + 2 * pos[1] + pos[2]

        def src(idx):
            return p_ref.at[idx]

        local = pltpu.make_async_copy(src(me), out_ref.at[me], local_sem)
        local.start()
        sends, recvs = [], []
        for k in range(1, N_DEV):
            bits = ((k >> 2) & 1, (k >> 1) & 1, k & 1)
            peer = tuple(1 - c if b else c for c, b in zip(pos, bits))
            peer_idx = 4 * peer[0] + 2 * peer[1] + peer[2]
            common = dict(send_sem=send_sems.at[k - 1], recv_sem=recv_sems.at[k - 1], device_id=peer,
                          device_id_type=pl.DeviceIdType.MESH)
            sends.append(pltpu.make_async_remote_copy(src_ref=src(peer_idx), dst_ref=out_ref.at[me], **common))
            recvs.append(pltpu.make_async_remote_copy(src_ref=src(peer_idx), dst_ref=out_ref.at[peer_idx], **common))
        for cp in sends:
            cp.start()
        for cp in recvs:
            cp.wait_recv()
        for cp in sends:
            cp.wait_send()
        local.wait()

    anyspace = pl.BlockSpec(memory_space=pl.ANY)
    return pl.pallas_call(
        body, name=name, in_specs=[anyspace], out_specs=anyspace,
        out_shape=jax.ShapeDtypeStruct((N_DEV,) + shape, p.dtype),
        scratch_shapes=[pltpu.SemaphoreType.DMA((N_DEV - 1,)), pltpu.SemaphoreType.DMA((N_DEV - 1,)),
                        pltpu.SemaphoreType.DMA],
    )(p)


def _adamw(parts, w, m, v):
    R, C = w.shape
    tr = _tile(R, 128, SUBLANE)
    c1 = 1.0 - ADAM_B1 ** ADAM_STEP
    c2 = 1.0 - ADAM_B2 ** ADAM_STEP

    def body(p_ref, w_ref, m_ref, v_ref, g_ref, d_ref, mo_ref, vo_ref):
        g = p_ref[0]
        for s in range(1, N_DEV):
            g = g + p_ref[s]
        mn = ADAM_B1 * m_ref[...] + (1.0 - ADAM_B1) * g
        vn = ADAM_B2 * v_ref[...] + (1.0 - ADAM_B2) * (g * g)
        m_hat = mn / c1
        v_hat = vn / c2
        g_ref[...] = g
        d_ref[...] = -ADAM_LR * (m_hat / (jnp.sqrt(v_hat) + ADAM_EPS) + ADAM_WD * w_ref[...])
        mo_ref[...] = mn
        vo_ref[...] = vn

    row = pl.BlockSpec((tr, C), lambda i: (i, 0))
    out = jax.ShapeDtypeStruct((R, C), F32)
    return _call(body, name="adamw", grid=(R // tr,),
                 in_specs=[pl.BlockSpec((N_DEV, tr, C), lambda i: (0, i, 0)), row, row, row],
                 out_specs=[row, row, row, row], out_shape=[out, out, out, out], sem=("parallel",))(parts, w, m, v)


_MATMUL_W = [("w_in_ab", 2), ("w_out_ab", 1), ("w_qkv_c", 2), ("w_out_c", 1), ("mem_wq", 1), ("mem_wk", 1),
             ("mem_wv", 1), ("mem_wo", 1), ("ffn_w1", 2), ("ffn_w3", 2), ("ffn_w2", 1)]
_SMALL_SHARDED = [("conv_w", 2), ("ln_g", 2), ("ln_b", 2)]
_REPLICATED = ["gmlp_ln_g", "gmlp_ln_b", "gmlp_w_s", "gmlp_b_s", "conv_b", "conv_gn_g", "conv_gn_b"]
_WEIGHTS = ["w_in_ab", "gmlp_ln_g", "gmlp_ln_b", "gmlp_w_s", "gmlp_b_s", "conv_w", "conv_b", "conv_gn_g",
            "conv_gn_b", "w_out_ab", "w_qkv_c", "w_out_c", "mem_wq", "mem_wk", "mem_wv", "mem_wo", "ffn_w1",
            "ffn_w3", "ffn_w2", "ln_g", "ln_b"]
_PACK_ORDER = [n for n, _ in _MATMUL_W] + [n for n, _ in _SMALL_SHARDED] + _REPLICATED
_PACK_ROW_MULT = 128


_FF_AXIS = {"ffn_w1": 2, "ffn_w3": 2, "ffn_w2": 1}


def _to_padded(name, a):
    axis = _FF_AXIS.get(name)
    if axis is None:
        return a
    widths = [(0, 0)] * a.ndim
    widths[axis] = (0, D_FF_SHARD_PAD - a.shape[axis])
    return jnp.pad(a, widths)


def _from_padded(name, a):
    axis = _FF_AXIS.get(name)
    return a if axis is None else lax.slice_in_dim(a, 0, D_FF_SHARD, axis=axis)


def _unshard(g, axis):
    full = jnp.moveaxis(g, 0, axis)
    shp = full.shape
    return full.reshape(shp[:axis] + (shp[axis] * shp[axis + 1],) + shp[axis + 2:])


def _split_shards(full, axis):
    shp = full.shape
    parts = full.reshape(shp[:axis] + (N_DEV, shp[axis] // N_DEV) + shp[axis + 1:])
    return jnp.moveaxis(parts, axis, 0)


def _pack_rows(flat_parts, lead=()):
    flat = jnp.concatenate(flat_parts, axis=-1)
    n = flat.shape[-1]
    per = PACK_COLS * _PACK_ROW_MULT
    total = -(-n // per) * per
    if total != n:
        flat = jnp.concatenate([flat, jnp.zeros(lead + (total - n,), flat.dtype)], axis=-1)
    return flat.reshape(lead + (total // PACK_COLS, PACK_COLS))


def _gather_weights(w):
    big = _pack_rows([w[n].astype(BF16).reshape(-1) for n, _ in _MATMUL_W])
    small = _pack_rows([w[n].reshape(-1) for n, _ in _SMALL_SHARDED])
    big_all = _allgather(big, "gather_matmul_weights").reshape(N_DEV, -1)
    small_all = _allgather(small, "gather_small_weights").reshape(N_DEV, -1)
    full = {n: w[n] for n in _REPLICATED}
    for table, src in ((_MATMUL_W, big_all), (_SMALL_SHARDED, small_all)):
        off = 0
        for n, axis in table:
            size = w[n].size
            full[n] = _unshard(src[:, off:off + size].reshape((N_DEV,) + w[n].shape), axis)
            off += size
    full["w_qkv_cT"] = jnp.swapaxes(full.pop("w_qkv_c"), 1, 2)
    full["ffn_w13"] = jnp.concatenate([full.pop("ffn_w1"), full.pop("ffn_w3")], axis=2)
    return full


def _pack_grads(grads):
    pieces = [_split_shards(grads[n], axis).reshape(N_DEV, -1) for n, axis in _MATMUL_W + _SMALL_SHARDED]
    pieces += [jnp.broadcast_to(grads[n].reshape(1, -1), (N_DEV, grads[n].size)) for n in _REPLICATED]
    return _pack_rows(pieces, lead=(N_DEV,))


def _pack_local(w):
    return _pack_rows([w[n].reshape(-1) for n in _PACK_ORDER])


def _unpack_local(packed, like):
    flat = packed.reshape(-1)
    out, off = {}, 0
    for n in _PACK_ORDER:
        out[n] = flat[off:off + like[n].size].reshape(like[n].shape)
        off += like[n].size
    return out


def kernel(x, mem, w_in_ab, gmlp_ln_g, gmlp_ln_b, gmlp_w_s, gmlp_b_s, conv_w, conv_b, conv_gn_g, conv_gn_b, w_out_ab, w_qkv_c, w_out_c, mem_wq, mem_wk, mem_wv, mem_wo, ffn_w1, ffn_w3, ffn_w2, ln_g, ln_b, loss_target, m_w_in_ab, m_gmlp_ln_g, m_gmlp_ln_b, m_gmlp_w_s, m_gmlp_b_s, m_conv_w, m_conv_b, m_conv_gn_g, m_conv_gn_b, m_w_out_ab, m_w_qkv_c, m_w_out_c, m_mem_wq, m_mem_wk, m_mem_wv, m_mem_wo, m_ffn_w1, m_ffn_w3, m_ffn_w2, m_ln_g, m_ln_b, v_w_in_ab, v_gmlp_ln_g, v_gmlp_ln_b, v_gmlp_w_s, v_gmlp_b_s, v_conv_w, v_conv_b, v_conv_gn_g, v_conv_gn_b, v_w_out_ab, v_w_qkv_c, v_w_out_c, v_mem_wq, v_mem_wk, v_mem_wv, v_mem_wo, v_ffn_w1, v_ffn_w3, v_ffn_w2, v_ln_g, v_ln_b):
    w = dict(w_in_ab=w_in_ab, gmlp_ln_g=gmlp_ln_g, gmlp_ln_b=gmlp_ln_b, gmlp_w_s=gmlp_w_s, gmlp_b_s=gmlp_b_s,
             conv_w=conv_w, conv_b=conv_b, conv_gn_g=conv_gn_g, conv_gn_b=conv_gn_b, w_out_ab=w_out_ab,
             w_qkv_c=w_qkv_c, w_out_c=w_out_c, mem_wq=mem_wq, mem_wk=mem_wk, mem_wv=mem_wv, mem_wo=mem_wo,
             ffn_w1=ffn_w1, ffn_w3=ffn_w3, ffn_w2=ffn_w2, ln_g=ln_g, ln_b=ln_b)
    m = dict(w_in_ab=m_w_in_ab, gmlp_ln_g=m_gmlp_ln_g, gmlp_ln_b=m_gmlp_ln_b, gmlp_w_s=m_gmlp_w_s,
             gmlp_b_s=m_gmlp_b_s, conv_w=m_conv_w, conv_b=m_conv_b, conv_gn_g=m_conv_gn_g,
             conv_gn_b=m_conv_gn_b, w_out_ab=m_w_out_ab, w_qkv_c=m_w_qkv_c, w_out_c=m_w_out_c, mem_wq=m_mem_wq,
             mem_wk=m_mem_wk, mem_wv=m_mem_wv, mem_wo=m_mem_wo, ffn_w1=m_ffn_w1, ffn_w3=m_ffn_w3,
             ffn_w2=m_ffn_w2, ln_g=m_ln_g, ln_b=m_ln_b)
    v = dict(w_in_ab=v_w_in_ab, gmlp_ln_g=v_gmlp_ln_g, gmlp_ln_b=v_gmlp_ln_b, gmlp_w_s=v_gmlp_w_s,
             gmlp_b_s=v_gmlp_b_s, conv_w=v_conv_w, conv_b=v_conv_b, conv_gn_g=v_conv_gn_g,
             conv_gn_b=v_conv_gn_b, w_out_ab=v_w_out_ab, w_qkv_c=v_w_qkv_c, w_out_c=v_w_out_c, mem_wq=v_mem_wq,
             mem_wk=v_mem_wk, mem_wv=v_mem_wv, mem_wo=v_mem_wo, ffn_w1=v_ffn_w1, ffn_w3=v_ffn_w3,
             ffn_w2=v_ffn_w2, ln_g=v_ln_g, ln_b=v_ln_b)

    w, m, v = ({n: _to_padded(n, t[n]) for n in _WEIGHTS} for t in (w, m, v))
    full = _gather_weights(w)
    loss_part, grad_x, grads = _local_step(x, mem, loss_target, full)
    loss = lax.psum(jnp.sum(loss_part), AXES)

    parts = _exchange(_pack_grads(grads), "exchange_grads")
    packed = _adamw(parts, _pack_local(w), _pack_local(m), _pack_local(v))
    g_o, d_o, m_o, v_o = ({n: _from_padded(n, a) for n, a in _unpack_local(t, w).items()} for t in packed)
    return (loss, grad_x, *[g_o[n] for n in _WEIGHTS], *[d_o[n] for n in _WEIGHTS],
            *[m_o[n] for n in _WEIGHTS], *[v_o[n] for n in _WEIGHTS])
```

```python
import functools

import jax
import jax.numpy as jnp
from jax import lax
from jax.experimental import pallas as pl
from jax.experimental.pallas import tpu as pltpu

F32 = jnp.float32
BF16 = jnp.bfloat16
_MXU = jnp.bfloat16

D_MODEL = 1024
DEPTH = 4
MEM_LEN = 256
CHUNK = 128
A_GROUPS = 4
A_WIDTH = 512
B_WIDTH = 512
GROUP_DIM = 128
CONV_WIDTH = 31
C_HEADS = 16
C_HEAD_DIM = 64
MEM_HEADS = 4
MEM_HEAD_DIM = 256
D_FF = 2816
ALPHA = (2.0 * DEPTH) ** 0.25
LN_EPS = 1e-5
SB_SCALE = C_HEAD_DIM ** -0.5
MEM_SCALE = MEM_HEAD_DIM ** -0.5

ADAM_LR = 0.001
ADAM_B1 = 0.9
ADAM_B2 = 0.999
ADAM_EPS = 1e-08
ADAM_WD = 0.01
ADAM_STEP = 10

N_DEV = 8
AXES = ("x", "y", "c")
LANE = 128
SUBLANE = 8
PACK_COLS = 1024
_VMEM_LIMIT = 56 * 1024 * 1024

SB_TQ = 256
SB_TK = 128
SB_HEADS = 2
SB_BWD_HEADS = 2
D_FF_SHARD = D_FF // 8
D_FF_SHARD_PAD = 384
CONV_ROWS = 256
CONV_PAD = 32


def _tile(n, cap, mult):
    best = None
    for d in range(mult, min(n, cap) + 1, mult):
        if n % d == 0:
            best = d
    return n if best is None else best


def _call(body, *, name, grid, in_specs, out_specs, out_shape, scratch=(), sem=None):
    return pl.pallas_call(
        body, name=name, grid=grid, in_specs=in_specs, out_specs=out_specs, out_shape=out_shape,
        scratch_shapes=list(scratch),
        compiler_params=pltpu.CompilerParams(dimension_semantics=sem, vmem_limit_bytes=_VMEM_LIMIT))


def _dg(a, b, ca, cb):
    return lax.dot_general(a.astype(_MXU), b.astype(_MXU), (((ca,), (cb,)), ((), ())),
                           preferred_element_type=F32)


@jax.custom_vjp
def _dot(a, b):
    return _dg(a, b, 1, 0)


_dot.defvjp(lambda a, b: (_dg(a, b, 1, 0), (a, b)),
            lambda r, g: (_dg(g, r[1], 1, 1), _dg(r[0], g, 0, 0)))


@jax.custom_vjp
def _dot_t(a, b):
    return _dg(a, b, 1, 1)


_dot_t.defvjp(lambda a, b: (_dg(a, b, 1, 1), (a, b)),
              lambda r, g: (_dg(g, r[1], 1, 0), _dg(g, r[0], 0, 0)))


def _norm(z, g, b):
    mu = jnp.mean(z, axis=-1, keepdims=True)
    zc = z - mu
    var = jnp.mean(zc * zc, axis=-1, keepdims=True)
    return zc * lax.rsqrt(var + LN_EPS) * g + b


def _gelu(x):
    return 0.5 * x * (1.0 + lax.erf(x * (0.5 ** 0.5)))


def _matmul(a, b, mode="nn", add=None, out_dtype=F32, la=None, lb=None, name="mm"):
    ash, bsh = a.shape[-2:], b.shape[-2:]
    if mode == "nn":
        (M, K), (K2, N) = ash, bsh
    elif mode == "nt":
        (M, K), (N, K2) = ash, bsh
    else:
        (K, M), (K2, N) = ash, bsh
    assert K == K2, (a.shape, b.shape, mode)
    tm = _tile(M, 512, LANE)
    tn = _tile(N, 1408, LANE)
    tk = _tile(K, 512 if mode == "tn" else 1408, LANE)
    nk = K // tk
    if mode == "tn":
        a_blk, a_idx = (tk, tm), (lambda i, j, k: (k, i))
    else:
        a_blk, a_idx = (tm, tk), (lambda i, j, k: (i, k))
    if mode == "nt":
        b_blk, b_idx = (tn, tk), (lambda i, j, k: (j, k))
    else:
        b_blk, b_idx = (tk, tn), (lambda i, j, k: (k, j))
    dims = {"nn": (1, 0), "nt": (1, 1), "tn": (0, 0)}[mode]

    def spec(blk, idx, lead):
        if lead is None:
            return pl.BlockSpec(blk, idx)
        return pl.BlockSpec((None,) + blk, lambda i, j, k: (lead,) + idx(i, j, k))

    has_add = add is not None

    def body(*refs):
        a_ref, b_ref = refs[0], refs[1]
        add_ref = refs[2] if has_add else None
        o_ref = refs[3] if has_add else refs[2]
        p = _dg(a_ref[...], b_ref[...], *dims)

        def finish(r):
            if has_add:
                r = r + add_ref[...].astype(F32)
            o_ref[...] = r.astype(out_dtype)

        if nk == 1:
            finish(p)
        else:
            acc = refs[-1]
            k = pl.program_id(2)

            @pl.when(k == 0)
            def _():
                acc[...] = p

            @pl.when(k > 0)
            def _():
                acc[...] += p

            @pl.when(k == nk - 1)
            def _():
                finish(acc[...])

    in_specs = [spec(a_blk, a_idx, la), spec(b_blk, b_idx, lb)]
    args = [a, b]
    if has_add:
        in_specs.append(pl.BlockSpec((tm, tn), lambda i, j, k: (i, j)))
        args.append(add)
    return _call(
        body, name=name, grid=(M // tm, N // tn, nk), in_specs=in_specs,
        out_specs=pl.BlockSpec((tm, tn), lambda i, j, k: (i, j)),
        out_shape=jax.ShapeDtypeStruct((M, N), out_dtype),
        scratch=[pltpu.VMEM((tm, tn), F32)] if nk > 1 else [],
        sem=("parallel", "parallel", "arbitrary"))(*args)


def _ln_fn(x, f, g, b):
    return _norm(ALPHA * x + f, g, b)


def _ln_fwd(x, f, g, b):
    T, D = x.shape
    tm = _tile(T, 256, SUBLANE)

    def body(x_ref, f_ref, g_ref, b_ref, o_ref):
        o_ref[...] = _ln_fn(x_ref[...], f_ref[...], g_ref[...], b_ref[...])

    row = pl.BlockSpec((tm, D), lambda i: (i, 0))
    par = pl.BlockSpec((1, D), lambda i: (0, 0))
    return _call(body, name="ln_fwd", grid=(T // tm,), in_specs=[row, row, par, par], out_specs=row,
                 out_shape=jax.ShapeDtypeStruct((T, D), F32), sem=("parallel",))(x, f, g, b)


def _ln_bwd(x, f, g, b, dy):
    T, D = x.shape
    tm = _tile(T, 256, SUBLANE)

    def body(x_ref, f_ref, g_ref, b_ref, dy_ref, dx_ref, df_ref, dg_ref, db_ref):
        _, vjp = jax.vjp(_ln_fn, x_ref[...], f_ref[...], g_ref[...], b_ref[...])
        dx, df, dg, db = vjp(dy_ref[...])
        dx_ref[...] = dx
        df_ref[...] = df.astype(BF16)

        @pl.when(pl.program_id(0) == 0)
        def _():
            dg_ref[...] = jnp.zeros_like(dg_ref)
            db_ref[...] = jnp.zeros_like(db_ref)

        dg_ref[...] += dg
        db_ref[...] += db

    row = pl.BlockSpec((tm, D), lambda i: (i, 0))
    par = pl.BlockSpec((1, D), lambda i: (0, 0))
    return _call(body, name="ln_bwd", grid=(T // tm,), in_specs=[row, row, par, par, row],
                 out_specs=[row, row, par, par],
                 out_shape=[jax.ShapeDtypeStruct((T, D), F32), jax.ShapeDtypeStruct((T, D), BF16),
                            jax.ShapeDtypeStruct((1, D), F32), jax.ShapeDtypeStruct((1, D), F32)],
                 sem=("arbitrary",))(x, f, g, b, dy)


def _loss_fwd(y, tgt):
    T, D = y.shape
    tm = _tile(T, 256, SUBLANE)

    def body(y_ref, t_ref, l_ref, dy_ref):
        d = y_ref[...] - t_ref[...]
        dy_ref[...] = d * (1.0 / D)

        @pl.when(pl.program_id(0) == 0)
        def _():
            l_ref[...] = jnp.zeros_like(l_ref)

        l_ref[...] += jnp.sum(d * d, axis=0, keepdims=True) * (0.5 / D)

    row = pl.BlockSpec((tm, D), lambda i: (i, 0))
    par = pl.BlockSpec((1, D), lambda i: (0, 0))
    return _call(body, name="loss", grid=(T // tm,), in_specs=[row, row], out_specs=[par, row],
                 out_shape=[jax.ShapeDtypeStruct((1, D), F32), jax.ShapeDtypeStruct((T, D), F32)],
                 sem=("arbitrary",))(y, tgt)


def _swiglu_fn(h1, h3):
    return h1 * jax.nn.sigmoid(h1) * h3


def _swiglu_fwd(h13):
    T, F2 = h13.shape
    F = F2 // 2
    tm = _tile(T, 256, SUBLANE)

    def body(h1_ref, h3_ref, o_ref):
        o_ref[...] = _swiglu_fn(h1_ref[...], h3_ref[...]).astype(BF16)

    return _call(body, name="swiglu_fwd", grid=(T // tm,),
                 in_specs=[pl.BlockSpec((tm, F), lambda i: (i, 0)), pl.BlockSpec((tm, F), lambda i: (i, 1))],
                 out_specs=pl.BlockSpec((tm, F), lambda i: (i, 0)),
                 out_shape=jax.ShapeDtypeStruct((T, F), BF16), sem=("parallel",))(h13, h13)


def _swiglu_bwd(h13, dact):
    T, F2 = h13.shape
    F = F2 // 2
    tm = _tile(T, 256, SUBLANE)

    def body(h1_ref, h3_ref, d_ref, o_ref):
        _, vjp = jax.vjp(_swiglu_fn, h1_ref[...], h3_ref[...])
        d1, d3 = vjp(d_ref[...])
        o_ref[:, :F] = d1.astype(BF16)
        o_ref[:, F:] = d3.astype(BF16)

    return _call(body, name="swiglu_bwd", grid=(T // tm,),
                 in_specs=[pl.BlockSpec((tm, F), lambda i: (i, 0)), pl.BlockSpec((tm, F), lambda i: (i, 1)),
                           pl.BlockSpec((tm, F), lambda i: (i, 0))],
                 out_specs=pl.BlockSpec((tm, F2), lambda i: (i, 0)),
                 out_shape=jax.ShapeDtypeStruct((T, F2), BF16), sem=("parallel",))(h13, h13, dact)


@jax.custom_vjp
def _chunkmix(wm, vn, bs):
    n = vn.shape[0] // CHUNK
    return jnp.concatenate([_dg(wm, vn[c * CHUNK:(c + 1) * CHUNK], 1, 0) + bs for c in range(n)], axis=0)


def _chunkmix_fwd(wm, vn, bs):
    return _chunkmix(wm, vn, bs), (wm, vn)


def _chunkmix_bwd(res, ct):
    wm, vn = res
    n = vn.shape[0] // CHUNK
    cts = [ct[c * CHUNK:(c + 1) * CHUNK] for c in range(n)]
    dvn = jnp.concatenate([_dg(wm, cts[c], 0, 0) for c in range(n)], axis=0)
    dwm = sum(_dg(cts[c], vn[c * CHUNK:(c + 1) * CHUNK], 1, 1) for c in range(n))
    dbs = sum(jnp.sum(cts[c], axis=1, keepdims=True) for c in range(n))
    return dwm, dvn, dbs


_chunkmix.defvjp(_chunkmix_fwd, _chunkmix_bwd)


def _gmlp_fn(u, v, lg, lb, ws, bs):
    ug = _gelu(u)
    vn = _norm(_gelu(v), lg, lb)
    r = lax.broadcasted_iota(jnp.int32, (CHUNK, CHUNK), 0)
    c = lax.broadcasted_iota(jnp.int32, (CHUNK, CHUNK), 1)
    wm = jnp.where(r >= c, ws, 0.0)
    return ug * _chunkmix(wm, vn, bs)


def _gmlp_specs(tm, order):
    def at(col0):
        return pl.BlockSpec((tm, GROUP_DIM), lambda *ids: (order(*ids)[0], col0 + order(*ids)[1]))
    par = pl.BlockSpec((1, GROUP_DIM), lambda *ids: (0, order(*ids)[1]))
    ws = pl.BlockSpec((None, CHUNK, CHUNK), lambda *ids: (order(*ids)[1], 0, 0))
    bs = pl.BlockSpec((None, CHUNK, 1), lambda *ids: (order(*ids)[1], 0, 0))
    return at, par, ws, bs


def _gmlp_fwd(hin, lg, lb, ws, bs):
    T = hin.shape[0]
    tm = _tile(T, 512, CHUNK)
    at, par, wsp, bsp = _gmlp_specs(tm, lambda i, g: (i, g))

    def body(u_ref, v_ref, lg_ref, lb_ref, ws_ref, bs_ref, o_ref):
        o_ref[...] = _gmlp_fn(u_ref[...], v_ref[...], lg_ref[...], lb_ref[...], ws_ref[...],
                              bs_ref[...]).astype(BF16)

    return _call(body, name="gmlp_fwd", grid=(T // tm, A_GROUPS),
                 in_specs=[at(0), at(A_GROUPS), par, par, wsp, bsp], out_specs=at(0),
                 out_shape=jax.ShapeDtypeStruct((T, A_WIDTH), BF16),
                 sem=("parallel", "parallel"))(hin, hin, lg, lb, ws, bs)


def _gmlp_bwd(hin, lg, lb, ws, bs, dyab):
    T = hin.shape[0]
    tm = _tile(T, 512, CHUNK)
    at, par, wsp, bsp = _gmlp_specs(tm, lambda g, i: (i, g))

    def body(u_ref, v_ref, lg_ref, lb_ref, ws_ref, bs_ref, dy_ref,
             du_ref, dv_ref, dlg_ref, dlb_ref, dws_ref, dbs_ref):
        _, vjp = jax.vjp(_gmlp_fn, u_ref[...], v_ref[...], lg_ref[...], lb_ref[...], ws_ref[...], bs_ref[...])
        du, dv, dlg, dlb, dws, dbs = vjp(dy_ref[...])
        du_ref[...] = du.astype(BF16)
        dv_ref[...] = dv.astype(BF16)

        @pl.when(pl.program_id(1) == 0)
        def _():
            dlg_ref[...] = jnp.zeros_like(dlg_ref)
            dlb_ref[...] = jnp.zeros_like(dlb_ref)
            dws_ref[...] = jnp.zeros_like(dws_ref)
            dbs_ref[...] = jnp.zeros_like(dbs_ref)

        dlg_ref[...] += dlg
        dlb_ref[...] += dlb
        dws_ref[...] += dws
        dbs_ref[...] += dbs

    half = jax.ShapeDtypeStruct((T, A_WIDTH), BF16)
    return _call(body, name="gmlp_bwd", grid=(A_GROUPS, T // tm),
                 in_specs=[at(0), at(A_GROUPS), par, par, wsp, bsp, at(0)],
                 out_specs=[at(0), at(0), par, par, wsp, bsp],
                 out_shape=[half, half, jax.ShapeDtypeStruct((1, A_WIDTH), F32),
                            jax.ShapeDtypeStruct((1, A_WIDTH), F32),
                            jax.ShapeDtypeStruct((A_GROUPS, CHUNK, CHUNK), F32),
                            jax.ShapeDtypeStruct((A_GROUPS, CHUNK, 1), F32)],
                 sem=("parallel", "arbitrary"))(hin, hin, lg, lb, ws, bs, dyab)


def _glu_fn(a, gt):
    return a * jax.nn.sigmoid(gt)


def _gn_silu_fn(c, cb, gg, gb):
    y = _norm(c + cb, gg, gb)
    return y * jax.nn.sigmoid(y)


def _conv_taps(w_ref, src_ref, row0, first):
    acc = None
    for k in range(CONV_WIDTH):
        term = w_ref[pl.ds(k, 1), :] * src_ref[pl.ds(row0 + first(k), CONV_ROWS), :]
        acc = term if acc is None else acc + term
    return acc


def _conv_specs(S, order):
    def at(col0):
        return pl.BlockSpec((S, GROUP_DIM), lambda *ids: (order(*ids)[0], col0 + order(*ids)[1]))
    par = pl.BlockSpec((1, GROUP_DIM), lambda *ids: (0, order(*ids)[1]))
    cw = pl.BlockSpec((CONV_WIDTH, GROUP_DIM), lambda *ids: (0, order(*ids)[1]))
    return at, par, cw


def _conv_fwd(hin, cw, cb, gg, gb, n_ex, S):
    T = hin.shape[0]
    at, par, cwp = _conv_specs(S, lambda e, g: (e, g))
    ngrp = B_WIDTH // GROUP_DIM
    lead = CONV_PAD - (CONV_WIDTH - 1)

    def body(a_ref, gt_ref, cw_ref, cb_ref, gg_ref, gb_ref, o_ref, pad_ref):
        pad_ref[0:CONV_PAD, :] = jnp.zeros((CONV_PAD, GROUP_DIM), F32)
        for r in range(S // CONV_ROWS):
            rows = pl.ds(r * CONV_ROWS, CONV_ROWS)
            pad_ref[pl.ds(CONV_PAD + r * CONV_ROWS, CONV_ROWS), :] = _glu_fn(a_ref[rows, :], gt_ref[rows, :])
        for r in range(S // CONV_ROWS):
            c = _conv_taps(cw_ref, pad_ref, r * CONV_ROWS, lambda k: lead + k)
            o_ref[pl.ds(r * CONV_ROWS, CONV_ROWS), :] = _gn_silu_fn(
                c, cb_ref[...], gg_ref[...], gb_ref[...]).astype(BF16)

    return _call(body, name="conv_fwd", grid=(n_ex, ngrp),
                 in_specs=[at(2 * A_GROUPS), at(2 * A_GROUPS + ngrp), cwp, par, par, par], out_specs=at(0),
                 out_shape=jax.ShapeDtypeStruct((T, B_WIDTH), BF16),
                 scratch=[pltpu.VMEM((S + CONV_PAD, GROUP_DIM), F32)],
                 sem=("parallel", "parallel"))(hin, hin, cw, cb, gg, gb)


def _conv_bwd(hin, cw, cb, gg, gb, dyab, n_ex, S):
    T = hin.shape[0]
    at, par, cwp = _conv_specs(S, lambda g, e: (e, g))
    ngrp = B_WIDTH // GROUP_DIM
    lead = CONV_PAD - (CONV_WIDTH - 1)
    nchunk = S // CONV_ROWS

    def body(a_ref, gt_ref, cw_ref, cb_ref, gg_ref, gb_ref, dy_ref,
             da_ref, dgt_ref, dcw_ref, dcb_ref, dgg_ref, dgb_ref, pad_ref, dcp_ref, wacc_ref, pacc_ref):
        @pl.when(pl.program_id(1) == 0)
        def _():
            wacc_ref[...] = jnp.zeros_like(wacc_ref)
            pacc_ref[...] = jnp.zeros_like(pacc_ref)

        pad_ref[0:CONV_PAD, :] = jnp.zeros((CONV_PAD, GROUP_DIM), F32)
        dcp_ref[S:S + CONV_PAD, :] = jnp.zeros((CONV_PAD, GROUP_DIM), F32)
        for r in range(nchunk):
            rows = pl.ds(r * CONV_ROWS, CONV_ROWS)
            pad_ref[pl.ds(CONV_PAD + r * CONV_ROWS, CONV_ROWS), :] = _glu_fn(a_ref[rows, :], gt_ref[rows, :])
        for r in range(nchunk):
            rows = pl.ds(r * CONV_ROWS, CONV_ROWS)
            c = _conv_taps(cw_ref, pad_ref, r * CONV_ROWS, lambda k: lead + k)
            _, vjp = jax.vjp(_gn_silu_fn, c, cb_ref[...], gg_ref[...], gb_ref[...])
            dc, dcb, dgg, dgb = vjp(dy_ref[rows, :])
            dcp_ref[rows, :] = dc
            pacc_ref[0:1, :] += dcb
            pacc_ref[1:2, :] += dgg
            pacc_ref[2:3, :] += dgb
        for r in range(nchunk):
            rows = pl.ds(r * CONV_ROWS, CONV_ROWS)
            dh = _conv_taps(cw_ref, dcp_ref, r * CONV_ROWS, lambda k: CONV_WIDTH - 1 - k)
            _, vjp = jax.vjp(_glu_fn, a_ref[rows, :], gt_ref[rows, :])
            da, dgt = vjp(dh)
            da_ref[rows, :] = da.astype(BF16)
            dgt_ref[rows, :] = dgt.astype(BF16)
            dc = dcp_ref[rows, :]
            for k in range(CONV_WIDTH):
                prod = dc * pad_ref[pl.ds(r * CONV_ROWS + lead + k, CONV_ROWS), :]
                wacc_ref[k] += jnp.sum(prod.reshape(CONV_ROWS // SUBLANE, SUBLANE, GROUP_DIM), axis=0)
        for k in range(CONV_WIDTH):
            dcw_ref[pl.ds(k, 1), :] = jnp.sum(wacc_ref[k], axis=0, keepdims=True)
        dcb_ref[...] = pacc_ref[0:1, :]
        dgg_ref[...] = pacc_ref[1:2, :]
        dgb_ref[...] = pacc_ref[2:3, :]

    half = jax.ShapeDtypeStruct((T, B_WIDTH), BF16)
    vec = jax.ShapeDtypeStruct((1, B_WIDTH), F32)
    return _call(body, name="conv_bwd", grid=(ngrp, n_ex),
                 in_specs=[at(2 * A_GROUPS), at(2 * A_GROUPS + ngrp), cwp, par, par, par, at(ngrp)],
                 out_specs=[at(0), at(0), cwp, par, par, par],
                 out_shape=[half, half, jax.ShapeDtypeStruct((CONV_WIDTH, B_WIDTH), F32), vec, vec, vec],
                 scratch=[pltpu.VMEM((S + CONV_PAD, GROUP_DIM), F32), pltpu.VMEM((S + CONV_PAD, GROUP_DIM), F32),
                          pltpu.VMEM((CONV_WIDTH, SUBLANE, GROUP_DIM), F32), pltpu.VMEM((SUBLANE, GROUP_DIM), F32)],
                 sem=("parallel", "arbitrary"))(hin, hin, cw, cb, gg, gb, dyab)


def _sb_consts():
    r = lax.broadcasted_iota(jnp.int32, (SB_TK + SUBLANE, 2 * SB_TK), 0)
    c = lax.broadcasted_iota(jnp.int32, (SB_TK + SUBLANE, 2 * SB_TK), 1) % SB_TK
    tail = r >= SB_TK
    u_after = jnp.where((c > r) | tail, 1.0, 0.0).astype(BF16)
    u_before = jnp.where((c < r) | tail, 1.0, 0.0).astype(BF16)
    s = lax.broadcasted_iota(jnp.int32, (SB_TK, SB_TQ), 0)
    t = lax.broadcasted_iota(jnp.int32, (SB_TK, SB_TQ), 1)
    masks = [(s + SB_TK * d) < t for d in range(SB_TQ // SB_TK)]
    return u_after, u_before, masks


def _sb_store_split(hl, h, j, x):
    hi = x.astype(BF16)
    hl[h, j, :SB_TK] = hi
    hl[h, j, SB_TK:] = (x - hi.astype(F32)).astype(BF16)


def _split_sum(u, x):
    hi = x.astype(BF16)
    lo = (x - hi.astype(F32)).astype(BF16)
    res = _dg(u, jnp.concatenate([hi, lo], axis=0), 1, 0)
    return res[:SB_TK], res[SB_TK:]


def _sb_sums(u, hl, ws, tot, h, j):
    res = _dg(u, hl[h, j], 1, 0)
    ws[h, j] = res[:SB_TK]
    tot[h, j] = res[SB_TK:]


def _sb_tiles(trips, per, fn, carry=None, descending=False):
    def trip(t, c):
        t = trips - 1 - t if descending else t
        for u in (reversed(range(per)) if descending else range(per)):
            if carry is None:
                fn(per * t + u)
            else:
                c = fn(per * t + u, c)
        return c

    return lax.fori_loop(0, trips, trip, 0 if carry is None else carry)


def _add8(x, row8):
    return (x.reshape(-1, SUBLANE, x.shape[-1]) + row8[None]).reshape(x.shape)


def _sb_blocked(dst, src, h, n, width):
    for t in range(n):
        dst[h, t] = src[h, :, t * width:(t + 1) * width].astype(dst.dtype)


def _sb_spec(S, nh):
    def at(part):
        return pl.BlockSpec((nh, C_HEAD_DIM, S), lambda e, h: (part * (C_HEADS // nh) + h, 0, e))
    return at


def _sb_fwd(qkvT, n_ex, S):
    T = qkvT.shape[-1]
    nq, nk = S // SB_TQ, S // SB_TK
    per = SB_TQ // SB_TK
    heads = range(SB_HEADS)

    def body(q_ref, k_ref, v_ref, o_ref, qb, kb, vb, ob, zs, hl, ws, tot, ab):
        for h in heads:
            _sb_blocked(qb, q_ref, h, nq, SB_TQ)
            _sb_blocked(kb, k_ref, h, nk, SB_TK)
            _sb_blocked(vb, v_ref, h, nk, SB_TK)
        u_after, _, masks = _sb_consts()

        def soft(h, j, mask):
            z = zs[h, j]
            sp = jnp.maximum(z, 0.0) + jnp.log1p(jnp.exp(-jnp.abs(z)))
            _sb_store_split(hl, h, j, -sp if mask is None else jnp.where(mask, -sp, 0.0))
            zs[h, j] = z - sp

        def weigh(h, j, carry, mask):
            att = jnp.exp(zs[h, j] + _add8(ws[h, j], carry))
            if mask is not None:
                att = jnp.where(mask, att, 0.0)
            ab[h, j] = att.astype(_MXU)
            return carry + tot[h, j]

        def qtile(i, _):
            qs = [qb[h, i] for h in heads]

            def scores(j):
                for h in heads:
                    zs[h, j] = _dg(kb[h, j], qs[h], 0, 0) * SB_SCALE

            _sb_tiles(i + 1, per, scores)
            _sb_tiles(i, per, lambda j: [soft(h, j, None) for h in heads])
            for d in range(per):
                for h in heads:
                    soft(h, per * i + d, masks[d])
            _sb_tiles(i + 1, per, lambda j: [_sb_sums(u_after, hl, ws, tot, h, j) for h in heads])
            carry = tuple(jnp.zeros((SUBLANE, SB_TQ), F32) for h in heads)
            for d in reversed(range(per)):
                carry = tuple(weigh(h, per * i + d, carry[h], masks[d]) for h in heads)
            _sb_tiles(i, per, lambda j, c: tuple(weigh(h, j, c[h], None) for h in heads), carry, descending=True)
            acc = tuple(jnp.zeros((C_HEAD_DIM, SB_TQ), F32) for h in heads)
            acc = _sb_tiles(i + 1, per, lambda j, a: tuple(a[h] + _dg(vb[h, j], ab[h, j], 1, 0) for h in heads), acc)
            for h in heads:
                ob[h, i] = acc[h]
            return 0

        lax.fori_loop(0, nq, qtile, 0)
        for h in heads:
            for i in range(nq):
                o_ref[h, :, i * SB_TQ:(i + 1) * SB_TQ] = ob[h, i].astype(BF16)

    at = _sb_spec(S, SB_HEADS)
    qshape, kshape = (SB_HEADS, nq, C_HEAD_DIM, SB_TQ), (SB_HEADS, nk, C_HEAD_DIM, SB_TK)
    tiles = (SB_HEADS, nk, SB_TK, SB_TQ)
    return _call(body, name="sb_fwd", grid=(n_ex, C_HEADS // SB_HEADS), in_specs=[at(0), at(1), at(2)],
                 out_specs=at(0), out_shape=jax.ShapeDtypeStruct((C_HEADS, C_HEAD_DIM, T), BF16),
                 scratch=[pltpu.VMEM(qshape, _MXU), pltpu.VMEM(kshape, _MXU), pltpu.VMEM(kshape, _MXU),
                          pltpu.VMEM(qshape, F32), pltpu.VMEM(tiles, F32),
                          pltpu.VMEM((SB_HEADS, nk, 2 * SB_TK, SB_TQ), BF16), pltpu.VMEM(tiles, F32),
                          pltpu.VMEM((SB_HEADS, nk, SUBLANE, SB_TQ), F32), pltpu.VMEM(tiles, _MXU)],
                 sem=("parallel", "parallel"))(qkvT, qkvT, qkvT)


def _sb_bwd(qkvT, doT, n_ex, S):
    T = qkvT.shape[-1]
    nq, nk = S // SB_TQ, S // SB_TK
    per = SB_TQ // SB_TK
    heads = range(SB_BWD_HEADS)

    def body(q_ref, k_ref, v_ref, do_ref, dq_ref, dk_ref, dv_ref, qb, kb, vb, dob, dqa, dka, dva, dl_s, sg_s):
        for h in heads:
            _sb_blocked(qb, q_ref, h, nq, SB_TQ)
            _sb_blocked(dob, do_ref, h, nq, SB_TQ)
            _sb_blocked(kb, k_ref, h, nk, SB_TK)
            _sb_blocked(vb, v_ref, h, nk, SB_TK)
        dka[...] = jnp.zeros_like(dka)
        dva[...] = jnp.zeros_like(dva)
        u_after, u_before, masks = _sb_consts()

        def rebuild(h, j, qi, gi, carry, mask):
            z = _dg(kb[h, j], qi, 0, 0) * SB_SCALE
            e = jnp.exp(-jnp.abs(z))
            sp = jnp.maximum(z, 0.0) + jnp.log1p(e)
            within, total = _split_sum(u_after, -sp if mask is None else jnp.where(mask, -sp, 0.0))
            att = jnp.exp((z - sp) + _add8(within, carry))
            r = 1.0 / (1.0 + e)
            sig = jnp.where(z >= 0.0, r, e * r)
            if mask is not None:
                att = jnp.where(mask, att, 0.0)
                sig = jnp.where(mask, sig, 0.0)
            dl_s[h, j] = _dg(vb[h, j], gi, 0, 0) * att
            sg_s[h, j] = sig
            dva[h, j] += _dg(gi, att, 1, 1)
            return carry + total

        def push(h, j, qi, carry, dq):
            dlogit, sig = dl_s[h, j], sg_s[h, j]
            within, total = _split_sum(u_before, dlogit)
            dz = (dlogit * (1.0 - sig) - sig * _add8(within, carry)) * SB_SCALE
            dka[h, j] += _dg(qi, dz, 1, 1)
            return carry + total, dq + _dg(kb[h, j], dz, 1, 0)

        def qtile(i, _):
            qs = [qb[h, i] for h in heads]
            gs = [dob[h, i] for h in heads]
            carry = tuple(jnp.zeros((SUBLANE, SB_TQ), F32) for h in heads)
            for d in reversed(range(per)):
                carry = tuple(rebuild(h, per * i + d, qs[h], gs[h], carry[h], masks[d]) for h in heads)
            _sb_tiles(i, per, lambda j, c: tuple(rebuild(h, j, qs[h], gs[h], c[h], None) for h in heads), carry,
                      descending=True)
            st = tuple((jnp.zeros((SUBLANE, SB_TQ), F32), jnp.zeros((C_HEAD_DIM, SB_TQ), F32)) for h in heads)
            st = _sb_tiles(i + 1, per, lambda j, s: tuple(push(h, j, qs[h], *s[h]) for h in heads), st)
            for h in heads:
                dqa[h, i] = st[h][1]
            return 0

        lax.fori_loop(0, nq, qtile, 0)
        for h in heads:
            for i in range(nq):
                dq_ref[h, :, i * SB_TQ:(i + 1) * SB_TQ] = dqa[h, i].astype(BF16)
            for j in range(nk):
                dk_ref[h, :, j * SB_TK:(j + 1) * SB_TK] = dka[h, j].astype(BF16)
                dv_ref[h, :, j * SB_TK:(j + 1) * SB_TK] = dva[h, j].astype(BF16)

    nh = SB_BWD_HEADS
    at = _sb_spec(S, nh)
    out = jax.ShapeDtypeStruct((C_HEADS, C_HEAD_DIM, T), BF16)
    qshape, kshape = (nh, nq, C_HEAD_DIM, SB_TQ), (nh, nk, C_HEAD_DIM, SB_TK)
    tiles = (nh, nk, SB_TK, SB_TQ)
    return _call(body, name="sb_bwd", grid=(n_ex, C_HEADS // nh), in_specs=[at(0), at(1), at(2), at(0)],
                 out_specs=[at(0), at(0), at(0)], out_shape=[out, out, out],
                 scratch=[pltpu.VMEM(qshape, _MXU), pltpu.VMEM(kshape, _MXU), pltpu.VMEM(kshape, _MXU),
                          pltpu.VMEM(qshape, _MXU), pltpu.VMEM(qshape, F32), pltpu.VMEM(kshape, F32),
                          pltpu.VMEM(kshape, F32), pltpu.VMEM(tiles, F32), pltpu.VMEM(tiles, F32)],
                 sem=("parallel", "parallel"))(qkvT, qkvT, qkvT, doT)


def _xattn_fn(q, k, v):
    s = _dot_t(q, k) * MEM_SCALE
    m = lax.stop_gradient(jnp.max(s, axis=-1, keepdims=True))
    p = jnp.exp(s - m)
    p = p / jnp.sum(p, axis=-1, keepdims=True)
    return _dot(p, v)


def _xattn_fwd(q, kk, vv, n_ex, S):
    T = q.shape[0]
    tq = _tile(S, 1024, SUBLANE)
    nqt = S // tq

    def body(q_ref, k_ref, v_ref, o_ref):
        o_ref[...] = _xattn_fn(q_ref[...], k_ref[...], v_ref[...]).astype(BF16)

    qs = pl.BlockSpec((tq, MEM_HEAD_DIM), lambda e, i, h: (e * nqt + i, h))
    ks = pl.BlockSpec((MEM_LEN, MEM_HEAD_DIM), lambda e, i, h: (e, h))
    return _call(body, name="xattn_fwd", grid=(n_ex, nqt, MEM_HEADS), in_specs=[qs, ks, ks], out_specs=qs,
                 out_shape=jax.ShapeDtypeStruct((T, D_MODEL), BF16),
                 sem=("parallel", "parallel", "parallel"))(q, kk, vv)


def _xattn_bwd(q, kk, vv, do, n_ex, S):
    T = q.shape[0]
    tq = _tile(S, 1024, SUBLANE)
    nqt = S // tq

    def body(q_ref, k_ref, v_ref, do_ref, dq_ref, dk_ref, dv_ref):
        _, vjp = jax.vjp(_xattn_fn, q_ref[...].astype(F32), k_ref[...].astype(F32), v_ref[...].astype(F32))
        dq, dk, dv = vjp(do_ref[...].astype(F32))
        dq_ref[...] = dq.astype(BF16)

        @pl.when(pl.program_id(2) == 0)
        def _():
            dk_ref[...] = jnp.zeros_like(dk_ref)
            dv_ref[...] = jnp.zeros_like(dv_ref)

        dk_ref[...] += dk
        dv_ref[...] += dv

    qs = pl.BlockSpec((tq, MEM_HEAD_DIM), lambda e, h, i: (e * nqt + i, h))
    ks = pl.BlockSpec((MEM_LEN, MEM_HEAD_DIM), lambda e, h, i: (e, h))
    kv = jax.ShapeDtypeStruct(kk.shape, F32)
    return _call(body, name="xattn_bwd", grid=(n_ex, MEM_HEADS, nqt), in_specs=[qs, ks, ks, qs],
                 out_specs=[qs, ks, ks], out_shape=[jax.ShapeDtypeStruct((T, D_MODEL), BF16), kv, kv],
                 sem=("parallel", "parallel", "arbitrary"))(q, kk, vv, do)


def _local_step(x, mem, tgt, W):
    n_ex, S, D = x.shape
    T = n_ex * S
    h = x.reshape(T, D)
    mem2 = mem.reshape(n_ex * MEM_LEN, D)
    row = lambda v: v.reshape(1, -1)
    saved = []
    for l in range(DEPTH):
        sv = dict(x0=h)
        if l % 2 == 0:
            e = l // 2
            hin = _matmul(h, W["w_in_ab"], lb=e, name="in_proj")
            ya = _gmlp_fwd(hin, row(W["gmlp_ln_g"][e]), row(W["gmlp_ln_b"][e]), W["gmlp_w_s"][e],
                           W["gmlp_b_s"][e][:, :, None])
            yb = _conv_fwd(hin, W["conv_w"][e], row(W["conv_b"][e]), row(W["conv_gn_g"][e]),
                           row(W["conv_gn_b"][e]), n_ex, S)
            yab = jnp.concatenate([ya, yb], axis=1)
            mix = _matmul(yab, W["w_out_ab"], lb=e, name="out_proj")
            sv.update(hin=hin, yab=yab)
        else:
            o = l // 2
            qkvT = _matmul(W["w_qkv_cT"], h, mode="nt", la=o, out_dtype=BF16, name="qkv_proj")
            yT = _sb_fwd(qkvT.reshape(3 * C_HEADS, C_HEAD_DIM, T), n_ex, S)
            mix = _matmul(yT.reshape(D, T), W["w_out_c"], mode="tn", lb=o, name="sb_out_proj")
            sv.update(qkvT=qkvT, yT=yT)
        x1 = _ln_fwd(h, mix, row(W["ln_g"][l, 0]), row(W["ln_b"][l, 0]))
        q = _matmul(x1, W["mem_wq"], lb=l, out_dtype=BF16, name="mem_q")
        kk = _matmul(mem2, W["mem_wk"], lb=l, out_dtype=BF16, name="mem_kv")
        vv = _matmul(mem2, W["mem_wv"], lb=l, out_dtype=BF16, name="mem_kv")
        oc = _xattn_fwd(q, kk, vv, n_ex, S)
        cross = _matmul(oc, W["mem_wo"], lb=l, name="mem_o")
        x2 = _ln_fwd(x1, cross, row(W["ln_g"][l, 1]), row(W["ln_b"][l, 1]))
        h13 = _matmul(x2, W["ffn_w13"], lb=l, name="ffn_in")
        act = _swiglu_fwd(h13)
        f = _matmul(act, W["ffn_w2"], lb=l, name="ffn_out")
        x3 = _ln_fwd(x2, f, row(W["ln_g"][l, 2]), row(W["ln_b"][l, 2]))
        sv.update(mix=mix, x1=x1, q=q, kk=kk, vv=vv, oc=oc, cross=cross, x2=x2, h13=h13, act=act, f=f)
        saved.append(sv)
        h = x3

    loss_part, dh = _loss_fwd(h, tgt.reshape(T, D))

    G = {k: [None] * n for k, n in [
        ("w_in_ab", 2), ("gmlp_ln_g", 2), ("gmlp_ln_b", 2), ("gmlp_w_s", 2), ("gmlp_b_s", 2), ("conv_w", 2),
        ("conv_b", 2), ("conv_gn_g", 2), ("conv_gn_b", 2), ("w_out_ab", 2), ("w_qkv_c", 2), ("w_out_c", 2),
        ("mem_wq", 4), ("mem_wk", 4), ("mem_wv", 4), ("mem_wo", 4), ("ffn_w1", 4), ("ffn_w3", 4),
        ("ffn_w2", 4), ("ln_g", 4), ("ln_b", 4)]}
    for l in reversed(range(DEPTH)):
        sv = saved[l]
        lng, lnb = [None] * 3, [None] * 3
        dx2, df, lng[2], lnb[2] = _ln_bwd(sv["x2"], sv["f"], row(W["ln_g"][l, 2]), row(W["ln_b"][l, 2]), dh)
        dact = _matmul(df, W["ffn_w2"], mode="nt", lb=l, name="ffn_out_dx")
        G["ffn_w2"][l] = _matmul(sv["act"], df, mode="tn", name="ffn_out_dw")
        dh13 = _swiglu_bwd(sv["h13"], dact)
        dx2 = _matmul(dh13, W["ffn_w13"], mode="nt", lb=l, add=dx2, name="ffn_in_dx")
        dw13 = _matmul(sv["x2"], dh13, mode="tn", name="ffn_in_dw")
        ff = dw13.shape[1] // 2
        G["ffn_w1"][l], G["ffn_w3"][l] = dw13[:, :ff], dw13[:, ff:]
        dx1, dcross, lng[1], lnb[1] = _ln_bwd(sv["x1"], sv["cross"], row(W["ln_g"][l, 1]),
                                              row(W["ln_b"][l, 1]), dx2)
        doc = _matmul(dcross, W["mem_wo"], mode="nt", lb=l, out_dtype=BF16, name="mem_o_dx")
        G["mem_wo"][l] = _matmul(sv["oc"], dcross, mode="tn", name="mem_o_dw")
        dq, dkk, dvv = _xattn_bwd(sv["q"], sv["kk"], sv["vv"], doc, n_ex, S)
        dx1 = _matmul(dq, W["mem_wq"], mode="nt", lb=l, add=dx1, name="mem_q_dx")
        G["mem_wq"][l] = _matmul(sv["x1"], dq, mode="tn", name="mem_q_dw")
        G["mem_wk"][l] = _matmul(mem2, dkk, mode="tn", name="mem_kv_dw")
        G["mem_wv"][l] = _matmul(mem2, dvv, mode="tn", name="mem_kv_dw")
        dx0, dmix, lng[0], lnb[0] = _ln_bwd(sv["x0"], sv["mix"], row(W["ln_g"][l, 0]), row(W["ln_b"][l, 0]), dx1)
        if l % 2 == 0:
            e = l // 2
            dyab = _matmul(dmix, W["w_out_ab"], mode="nt", lb=e, name="out_proj_dx")
            G["w_out_ab"][e] = _matmul(sv["yab"], dmix, mode="tn", name="out_proj_dw")
            du, dv, dlg, dlb, dws, dbs = _gmlp_bwd(
                sv["hin"], row(W["gmlp_ln_g"][e]), row(W["gmlp_ln_b"][e]), W["gmlp_w_s"][e],
                W["gmlp_b_s"][e][:, :, None], dyab)
            da, dgt, dcw, dcb, dgg, dgb = _conv_bwd(
                sv["hin"], W["conv_w"][e], row(W["conv_b"][e]), row(W["conv_gn_g"][e]),
                row(W["conv_gn_b"][e]), dyab, n_ex, S)
            dhin = jnp.concatenate([du, dv, da, dgt], axis=1)
            dh = _matmul(dhin, W["w_in_ab"], mode="nt", lb=e, add=dx0, name="in_proj_dx")
            G["w_in_ab"][e] = _matmul(sv["x0"], dhin, mode="tn", name="in_proj_dw")
            G["gmlp_ln_g"][e], G["gmlp_ln_b"][e] = dlg[0], dlb[0]
            G["gmlp_w_s"][e], G["gmlp_b_s"][e] = dws, dbs[:, :, 0]
            G["conv_w"][e], G["conv_b"][e], G["conv_gn_g"][e], G["conv_gn_b"][e] = dcw, dcb[0], dgg[0], dgb[0]
        else:
            o = l // 2
            dyT = _matmul(W["w_out_c"], dmix, mode="nt", la=o, out_dtype=BF16, name="sb_out_proj_dx")
            G["w_out_c"][o] = _matmul(sv["yT"].reshape(D, T), dmix, name="sb_out_proj_dw")
            dqT, dkT, dvT = _sb_bwd(sv["qkvT"].reshape(3 * C_HEADS, C_HEAD_DIM, T),
                                    dyT.reshape(C_HEADS, C_HEAD_DIM, T), n_ex, S)
            dqkvT = jnp.concatenate([dqT, dkT, dvT], axis=0).reshape(3 * D, T)
            dh = _matmul(dqkvT, W["w_qkv_cT"], mode="tn", lb=o, add=dx0, name="qkv_proj_dx")
            G["w_qkv_c"][o] = _matmul(dqkvT, sv["x0"], name="qkv_proj_dw").T
        G["ln_g"][l] = jnp.concatenate(lng, axis=0)
        G["ln_b"][l] = jnp.concatenate(lnb, axis=0)
    grads = {k: jnp.stack(v, axis=0) for k, v in G.items()}
    return loss_part, dh.reshape(n_ex, S, D), grads


def _allgather(p, name):
    def body(p_ref, out_ref, send_sems, recv_sems, local_sem):
        x, y, c = (lax.axis_index(a) for a in AXES)
        me, sibling = (x, y, c), (x, y, 1 - c)
        chips = [(1 - x, y), (x, 1 - y), (1 - x, 1 - y)]

        def rows(px, py, pc):
            return out_ref.at[4 * px + 2 * py + pc]

        def copy(k, block, to, src=None):
            return pltpu.make_async_remote_copy(
                src_ref=rows(*block) if src is None else src, dst_ref=rows(*block), send_sem=send_sems.at[k],
                recv_sem=recv_sems.at[k], device_id=to, device_id_type=pl.DeviceIdType.MESH)

        mine = pltpu.make_async_copy(p_ref, rows(*me), local_sem)
        mine.start()
        first = [copy(0, me, sibling, src=p_ref)]
        first += [copy(1 + j, me, (*chip, c), src=p_ref) for j, chip in enumerate(chips)]
        for cp in first:
            cp.start()
        passed = [copy(4 + j, (*chip, c), sibling) for j, chip in enumerate(chips)]
        for j, chip in enumerate(chips):
            copy(1 + j, (*chip, c), me).wait_recv()
            passed[j].start()
        copy(0, sibling, me).wait_recv()
        for j, chip in enumerate(chips):
            copy(4 + j, (*chip, 1 - c), me).wait_recv()
        for cp in first + passed:
            cp.wait_send()
        mine.wait()

    anyspace = pl.BlockSpec(memory_space=pl.ANY)
    return pl.pallas_call(
        body, name=name, in_specs=[anyspace], out_specs=anyspace,
        out_shape=jax.ShapeDtypeStruct((N_DEV,) + p.shape, p.dtype),
        scratch_shapes=[pltpu.SemaphoreType.DMA((N_DEV - 1,)), pltpu.SemaphoreType.DMA((N_DEV - 1,)),
                        pltpu.SemaphoreType.DMA],
    )(p)


N_CHIP = 4
N_CORE = 2


def _pair_swap(p, name):
    def body(p_ref, out_ref, send_sem, recv_sem):
        x, y, c = (lax.axis_index(a) for a in AXES)
        cp = pltpu.make_async_remote_copy(src_ref=p_ref.at[1 - c], dst_ref=out_ref, send_sem=send_sem,
                                          recv_sem=recv_sem, device_id=(x, y, 1 - c),
                                          device_id_type=pl.DeviceIdType.MESH)
        cp.start()
        cp.wait()

    anyspace = pl.BlockSpec(memory_space=pl.ANY)
    return pl.pallas_call(
        body, name=name, in_specs=[anyspace], out_specs=anyspace,
        out_shape=jax.ShapeDtypeStruct(p.shape[1:], p.dtype),
        scratch_shapes=[pltpu.SemaphoreType.DMA, pltpu.SemaphoreType.DMA],
    )(p)


def _pair_add(p, s):
    _, n, R, C = p.shape
    tr = _tile(R, 128, SUBLANE)

    def body(p_ref, s_ref, o_ref):
        o_ref[...] = p_ref[lax.axis_index("c")] + s_ref[...]

    return _call(body, name="pair_add", grid=(n, R // tr),
                 in_specs=[pl.BlockSpec((N_CORE, None, tr, C), lambda k, i: (0, k, i, 0)),
                           pl.BlockSpec((None, tr, C), lambda k, i: (k, i, 0))],
                 out_specs=pl.BlockSpec((None, tr, C), lambda k, i: (k, i, 0)),
                 out_shape=jax.ShapeDtypeStruct(s.shape, s.dtype), sem=("parallel", "parallel"))(p, s)


def _chip_exchange(q, name):
    def body(q_ref, out_ref, send_sems, recv_sems, local_sem):
        x, y, c = (lax.axis_index(a) for a in AXES)
        mine = 2 * x + y
        local = pltpu.make_async_copy(q_ref.at[mine], out_ref.at[mine], local_sem)
        local.start()
        sends, recvs = [], []
        for j in range(1, N_CHIP):
            px = 1 - x if (j >> 1) & 1 else x
            py = 1 - y if j & 1 else y
            theirs = 2 * px + py
            common = dict(send_sem=send_sems.at[j - 1], recv_sem=recv_sems.at[j - 1], device_id=(px, py, c),
                          device_id_type=pl.DeviceIdType.MESH)
            sends.append(pltpu.make_async_remote_copy(src_ref=q_ref.at[theirs], dst_ref=out_ref.at[mine], **common))
            recvs.append(pltpu.make_async_remote_copy(src_ref=q_ref.at[theirs], dst_ref=out_ref.at[theirs], **common))
        for cp in sends:
            cp.start()
        for cp in recvs:
            cp.wait_recv()
        for cp in sends:
            cp.wait_send()
        local.wait()

    anyspace = pl.BlockSpec(memory_space=pl.ANY)
    return pl.pallas_call(
        body, name=name, in_specs=[anyspace], out_specs=anyspace,
        out_shape=jax.ShapeDtypeStruct(q.shape, q.dtype),
        scratch_shapes=[pltpu.SemaphoreType.DMA((N_CHIP - 1,)), pltpu.SemaphoreType.DMA((N_CHIP - 1,)),
                        pltpu.SemaphoreType.DMA],
    )(q)


def _adamw(parts, w, m, v):
    R, C = w.shape
    n_parts = parts.shape[0]
    tr = _tile(R, 128, SUBLANE)
    c1 = 1.0 - ADAM_B1 ** ADAM_STEP
    c2 = 1.0 - ADAM_B2 ** ADAM_STEP

    def body(p_ref, w_ref, m_ref, v_ref, g_ref, d_ref, mo_ref, vo_ref):
        g = p_ref[0]
        for s in range(1, n_parts):
            g = g + p_ref[s]
        mn = ADAM_B1 * m_ref[...] + (1.0 - ADAM_B1) * g
        vn = ADAM_B2 * v_ref[...] + (1.0 - ADAM_B2) * (g * g)
        m_hat = mn / c1
        v_hat = vn / c2
        g_ref[...] = g
        d_ref[...] = -ADAM_LR * (m_hat / (jnp.sqrt(v_hat) + ADAM_EPS) + ADAM_WD * w_ref[...])
        mo_ref[...] = mn
        vo_ref[...] = vn

    row = pl.BlockSpec((tr, C), lambda i: (i, 0))
    out = jax.ShapeDtypeStruct((R, C), F32)
    return _call(body, name="adamw", grid=(R // tr,),
                 in_specs=[pl.BlockSpec((n_parts, tr, C), lambda i: (0, i, 0)), row, row, row],
                 out_specs=[row, row, row, row], out_shape=[out, out, out, out], sem=("parallel",))(parts, w, m, v)


_MATMUL_W = [("w_in_ab", 2), ("w_out_ab", 1), ("w_qkv_c", 2), ("w_out_c", 1), ("mem_wq", 1), ("mem_wk", 1),
             ("mem_wv", 1), ("mem_wo", 1), ("ffn_w1", 2), ("ffn_w3", 2), ("ffn_w2", 1)]
_SMALL_SHARDED = [("conv_w", 2), ("ln_g", 2), ("ln_b", 2)]
_REPLICATED = ["gmlp_ln_g", "gmlp_ln_b", "gmlp_w_s", "gmlp_b_s", "conv_b", "conv_gn_g", "conv_gn_b"]
_WEIGHTS = ["w_in_ab", "gmlp_ln_g", "gmlp_ln_b", "gmlp_w_s", "gmlp_b_s", "conv_w", "conv_b", "conv_gn_g",
            "conv_gn_b", "w_out_ab", "w_qkv_c", "w_out_c", "mem_wq", "mem_wk", "mem_wv", "mem_wo", "ffn_w1",
            "ffn_w3", "ffn_w2", "ln_g", "ln_b"]
_PACK_ORDER = [n for n, _ in _MATMUL_W] + [n for n, _ in _SMALL_SHARDED] + _REPLICATED
_PACK_ROW_MULT = 128


_FF_AXIS = {"ffn_w1": 2, "ffn_w3": 2, "ffn_w2": 1}


def _to_padded(name, a):
    axis = _FF_AXIS.get(name)
    if axis is None:
        return a
    widths = [(0, 0)] * a.ndim
    widths[axis] = (0, D_FF_SHARD_PAD - a.shape[axis])
    return jnp.pad(a, widths)


def _from_padded(name, a):
    axis = _FF_AXIS.get(name)
    return a if axis is None else lax.slice_in_dim(a, 0, D_FF_SHARD, axis=axis)


def _unshard(g, axis):
    full = jnp.moveaxis(g, 0, axis)
    shp = full.shape
    return full.reshape(shp[:axis] + (shp[axis] * shp[axis + 1],) + shp[axis + 2:])


def _split_shards(full, axis):
    shp = full.shape
    parts = full.reshape(shp[:axis] + (N_DEV, shp[axis] // N_DEV) + shp[axis + 1:])
    return jnp.moveaxis(parts, axis, 0)


def _pack_rows(flat_parts, lead=()):
    flat = jnp.concatenate(flat_parts, axis=-1)
    n = flat.shape[-1]
    per = PACK_COLS * _PACK_ROW_MULT
    total = -(-n // per) * per
    if total != n:
        flat = jnp.concatenate([flat, jnp.zeros(lead + (total - n,), flat.dtype)], axis=-1)
    return flat.reshape(lead + (total // PACK_COLS, PACK_COLS))


def _gather_weights(w):
    big = _pack_rows([w[n].astype(BF16).reshape(-1) for n, _ in _MATMUL_W])
    small = _pack_rows([w[n].reshape(-1) for n, _ in _SMALL_SHARDED])
    big_all = _allgather(big, "gather_matmul_weights").reshape(N_DEV, -1)
    small_all = _allgather(small, "gather_small_weights").reshape(N_DEV, -1)
    full = {n: w[n] for n in _REPLICATED}
    for table, src in ((_MATMUL_W, big_all), (_SMALL_SHARDED, small_all)):
        off = 0
        for n, axis in table:
            size = w[n].size
            full[n] = _unshard(src[:, off:off + size].reshape((N_DEV,) + w[n].shape), axis)
            off += size
    full["w_qkv_cT"] = jnp.swapaxes(full.pop("w_qkv_c"), 1, 2)
    full["ffn_w13"] = jnp.concatenate([full.pop("ffn_w1"), full.pop("ffn_w3")], axis=2)
    return full


def _pack_grads(grads):
    pieces = [_split_shards(grads[n], axis).reshape(N_DEV, -1) for n, axis in _MATMUL_W + _SMALL_SHARDED]
    pieces += [jnp.broadcast_to(grads[n].reshape(1, -1), (N_DEV, grads[n].size)) for n in _REPLICATED]
    return _pack_rows(pieces, lead=(N_DEV,))


def _pack_local(w):
    return _pack_rows([w[n].reshape(-1) for n in _PACK_ORDER])


def _unpack_local(packed, like):
    flat = packed.reshape(-1)
    out, off = {}, 0
    for n in _PACK_ORDER:
        out[n] = flat[off:off + like[n].size].reshape(like[n].shape)
        off += like[n].size
    return out


def kernel(x, mem, w_in_ab, gmlp_ln_g, gmlp_ln_b, gmlp_w_s, gmlp_b_s, conv_w, conv_b, conv_gn_g, conv_gn_b, w_out_ab, w_qkv_c, w_out_c, mem_wq, mem_wk, mem_wv, mem_wo, ffn_w1, ffn_w3, ffn_w2, ln_g, ln_b, loss_target, m_w_in_ab, m_gmlp_ln_g, m_gmlp_ln_b, m_gmlp_w_s, m_gmlp_b_s, m_conv_w, m_conv_b, m_conv_gn_g, m_conv_gn_b, m_w_out_ab, m_w_qkv_c, m_w_out_c, m_mem_wq, m_mem_wk, m_mem_wv, m_mem_wo, m_ffn_w1, m_ffn_w3, m_ffn_w2, m_ln_g, m_ln_b, v_w_in_ab, v_gmlp_ln_g, v_gmlp_ln_b, v_gmlp_w_s, v_gmlp_b_s, v_conv_w, v_conv_b, v_conv_gn_g, v_conv_gn_b, v_w_out_ab, v_w_qkv_c, v_w_out_c, v_mem_wq, v_mem_wk, v_mem_wv, v_mem_wo, v_ffn_w1, v_ffn_w3, v_ffn_w2, v_ln_g, v_ln_b):
    w = dict(w_in_ab=w_in_ab, gmlp_ln_g=gmlp_ln_g, gmlp_ln_b=gmlp_ln_b, gmlp_w_s=gmlp_w_s, gmlp_b_s=gmlp_b_s,
             conv_w=conv_w, conv_b=conv_b, conv_gn_g=conv_gn_g, conv_gn_b=conv_gn_b, w_out_ab=w_out_ab,
             w_qkv_c=w_qkv_c, w_out_c=w_out_c, mem_wq=mem_wq, mem_wk=mem_wk, mem_wv=mem_wv, mem_wo=mem_wo,
             ffn_w1=ffn_w1, ffn_w3=ffn_w3, ffn_w2=ffn_w2, ln_g=ln_g, ln_b=ln_b)
    m = dict(w_in_ab=m_w_in_ab, gmlp_ln_g=m_gmlp_ln_g, gmlp_ln_b=m_gmlp_ln_b, gmlp_w_s=m_gmlp_w_s,
             gmlp_b_s=m_gmlp_b_s, conv_w=m_conv_w, conv_b=m_conv_b, conv_gn_g=m_conv_gn_g,
             conv_gn_b=m_conv_gn_b, w_out_ab=m_w_out_ab, w_qkv_c=m_w_qkv_c, w_out_c=m_w_out_c, mem_wq=m_mem_wq,
             mem_wk=m_mem_wk, mem_wv=m_mem_wv, mem_wo=m_mem_wo, ffn_w1=m_ffn_w1, ffn_w3=m_ffn_w3,
             ffn_w2=m_ffn_w2, ln_g=m_ln_g, ln_b=m_ln_b)
    v = dict(w_in_ab=v_w_in_ab, gmlp_ln_g=v_gmlp_ln_g, gmlp_ln_b=v_gmlp_ln_b, gmlp_w_s=v_gmlp_w_s,
             gmlp_b_s=v_gmlp_b_s, conv_w=v_conv_w, conv_b=v_conv_b, conv_gn_g=v_conv_gn_g,
             conv_gn_b=v_conv_gn_b, w_out_ab=v_w_out_ab, w_qkv_c=v_w_qkv_c, w_out_c=v_w_out_c, mem_wq=v_mem_wq,
             mem_wk=v_mem_wk, mem_wv=v_mem_wv, mem_wo=v_mem_wo, ffn_w1=v_ffn_w1, ffn_w3=v_ffn_w3,
             ffn_w2=v_ffn_w2, ln_g=v_ln_g, ln_b=v_ln_b)

    w, m, v = ({n: _to_padded(n, t[n]) for n in _WEIGHTS} for t in (w, m, v))
    full = _gather_weights(w)
    loss_part, grad_x, grads = _local_step(x, mem, loss_target, full)
    loss = lax.psum(jnp.sum(loss_part), AXES)

    by_dev = _pack_grads(grads)
    by_core = jnp.swapaxes(by_dev.reshape((N_CHIP, N_CORE) + by_dev.shape[1:]), 0, 1)
    chip_sum = _pair_add(by_core, _pair_swap(by_core, "swap_grads_in_chip"))
    parts = _chip_exchange(chip_sum, "exchange_grads")
    packed = _adamw(parts, _pack_local(w), _pack_local(m), _pack_local(v))
    g_o, d_o, m_o, v_o = ({n: _from_padded(n, a) for n, a in _unpack_local(t, w).items()} for t in packed)
    return (loss, grad_x, *[g_o[n] for n in _WEIGHTS], *[d_o[n] for n in _WEIGHTS],
            *[m_o[n] for n in _WEIGHTS], *[v_o[n] for n in _WEIGHTS])
```

```python
import functools

import jax
import jax.numpy as jnp
from jax import lax
from jax.experimental import pallas as pl
from jax.experimental.pallas import tpu as pltpu

F32 = jnp.float32
BF16 = jnp.bfloat16
_MXU = jnp.bfloat16

D_MODEL = 1024
DEPTH = 4
MEM_LEN = 256
CHUNK = 128
A_GROUPS = 4
A_WIDTH = 512
B_WIDTH = 512
GROUP_DIM = 128
CONV_WIDTH = 31
C_HEADS = 16
C_HEAD_DIM = 64
MEM_HEADS = 4
MEM_HEAD_DIM = 256
D_FF = 2816
ALPHA = (2.0 * DEPTH) ** 0.25
LN_EPS = 1e-5
SB_SCALE = C_HEAD_DIM ** -0.5
MEM_SCALE = MEM_HEAD_DIM ** -0.5

ADAM_LR = 0.001
ADAM_B1 = 0.9
ADAM_B2 = 0.999
ADAM_EPS = 1e-08
ADAM_WD = 0.01
ADAM_STEP = 10

N_DEV = 8
AXES = ("x", "y", "c")
LANE = 128
SUBLANE = 8
PACK_COLS = 1024
_VMEM_LIMIT = 56 * 1024 * 1024

SB_TQ = 256
SB_TK = 128
SB_HEADS = 2
SB_BWD_HEADS = 2
D_FF_SHARD = D_FF // 8
D_FF_SHARD_PAD = 384
CONV_ROWS = 256
CONV_PAD = 32


def _tile(n, cap, mult):
    best = None
    for d in range(mult, min(n, cap) + 1, mult):
        if n % d == 0:
            best = d
    return n if best is None else best


def _call(body, *, name, grid, in_specs, out_specs, out_shape, scratch=(), sem=None):
    return pl.pallas_call(
        body, name=name, grid=grid, in_specs=in_specs, out_specs=out_specs, out_shape=out_shape,
        scratch_shapes=list(scratch),
        compiler_params=pltpu.CompilerParams(dimension_semantics=sem, vmem_limit_bytes=_VMEM_LIMIT))


def _dg(a, b, ca, cb):
    return lax.dot_general(a.astype(_MXU), b.astype(_MXU), (((ca,), (cb,)), ((), ())),
                           preferred_element_type=F32)


@jax.custom_vjp
def _dot(a, b):
    return _dg(a, b, 1, 0)


_dot.defvjp(lambda a, b: (_dg(a, b, 1, 0), (a, b)),
            lambda r, g: (_dg(g, r[1], 1, 1), _dg(r[0], g, 0, 0)))


@jax.custom_vjp
def _dot_t(a, b):
    return _dg(a, b, 1, 1)


_dot_t.defvjp(lambda a, b: (_dg(a, b, 1, 1), (a, b)),
              lambda r, g: (_dg(g, r[1], 1, 0), _dg(g, r[0], 0, 0)))


def _norm(z, g, b):
    mu = jnp.mean(z, axis=-1, keepdims=True)
    zc = z - mu
    var = jnp.mean(zc * zc, axis=-1, keepdims=True)
    return zc * lax.rsqrt(var + LN_EPS) * g + b


def _gelu(x):
    return 0.5 * x * (1.0 + lax.erf(x * (0.5 ** 0.5)))


def _matmul(a, b, mode="nn", add=None, out_dtype=F32, la=None, lb=None, name="mm"):
    ash, bsh = a.shape[-2:], b.shape[-2:]
    if mode == "nn":
        (M, K), (K2, N) = ash, bsh
    elif mode == "nt":
        (M, K), (N, K2) = ash, bsh
    else:
        (K, M), (K2, N) = ash, bsh
    assert K == K2, (a.shape, b.shape, mode)
    tm = _tile(M, 512, LANE)
    tn = _tile(N, 1408, LANE)
    tk = _tile(K, 512 if mode == "tn" else 1408, LANE)
    nk = K // tk
    if mode == "tn":
        a_blk, a_idx = (tk, tm), (lambda i, j, k: (k, i))
    else:
        a_blk, a_idx = (tm, tk), (lambda i, j, k: (i, k))
    if mode == "nt":
        b_blk, b_idx = (tn, tk), (lambda i, j, k: (j, k))
    else:
        b_blk, b_idx = (tk, tn), (lambda i, j, k: (k, j))
    dims = {"nn": (1, 0), "nt": (1, 1), "tn": (0, 0)}[mode]

    def spec(blk, idx, lead):
        if lead is None:
            return pl.BlockSpec(blk, idx)
        return pl.BlockSpec((None,) + blk, lambda i, j, k: (lead,) + idx(i, j, k))

    has_add = add is not None

    def body(*refs):
        a_ref, b_ref = refs[0], refs[1]
        add_ref = refs[2] if has_add else None
        o_ref = refs[3] if has_add else refs[2]
        p = _dg(a_ref[...], b_ref[...], *dims)

        def finish(r):
            if has_add:
                r = r + add_ref[...].astype(F32)
            o_ref[...] = r.astype(out_dtype)

        if nk == 1:
            finish(p)
        else:
            acc = refs[-1]
            k = pl.program_id(2)

            @pl.when(k == 0)
            def _():
                acc[...] = p

            @pl.when(k > 0)
            def _():
                acc[...] += p

            @pl.when(k == nk - 1)
            def _():
                finish(acc[...])

    in_specs = [spec(a_blk, a_idx, la), spec(b_blk, b_idx, lb)]
    args = [a, b]
    if has_add:
        in_specs.append(pl.BlockSpec((tm, tn), lambda i, j, k: (i, j)))
        args.append(add)
    return _call(
        body, name=name, grid=(M // tm, N // tn, nk), in_specs=in_specs,
        out_specs=pl.BlockSpec((tm, tn), lambda i, j, k: (i, j)),
        out_shape=jax.ShapeDtypeStruct((M, N), out_dtype),
        scratch=[pltpu.VMEM((tm, tn), F32)] if nk > 1 else [],
        sem=("parallel", "parallel", "arbitrary"))(*args)


def _ln_fn(x, f, g, b):
    return _norm(ALPHA * x + f, g, b)


def _ln_fwd(x, f, g, b):
    T, D = x.shape
    tm = _tile(T, 256, SUBLANE)

    def body(x_ref, f_ref, g_ref, b_ref, o_ref):
        o_ref[...] = _ln_fn(x_ref[...], f_ref[...], g_ref[...], b_ref[...])

    row = pl.BlockSpec((tm, D), lambda i: (i, 0))
    par = pl.BlockSpec((1, D), lambda i: (0, 0))
    return _call(body, name="ln_fwd", grid=(T // tm,), in_specs=[row, row, par, par], out_specs=row,
                 out_shape=jax.ShapeDtypeStruct((T, D), F32), sem=("parallel",))(x, f, g, b)


def _ln_bwd(x, f, g, b, dy):
    T, D = x.shape
    tm = _tile(T, 256, SUBLANE)

    def body(x_ref, f_ref, g_ref, b_ref, dy_ref, dx_ref, df_ref, dg_ref, db_ref):
        _, vjp = jax.vjp(_ln_fn, x_ref[...], f_ref[...], g_ref[...], b_ref[...])
        dx, df, dg, db = vjp(dy_ref[...])
        dx_ref[...] = dx
        df_ref[...] = df.astype(BF16)

        @pl.when(pl.program_id(0) == 0)
        def _():
            dg_ref[...] = jnp.zeros_like(dg_ref)
            db_ref[...] = jnp.zeros_like(db_ref)

        dg_ref[...] += dg
        db_ref[...] += db

    row = pl.BlockSpec((tm, D), lambda i: (i, 0))
    par = pl.BlockSpec((1, D), lambda i: (0, 0))
    return _call(body, name="ln_bwd", grid=(T // tm,), in_specs=[row, row, par, par, row],
                 out_specs=[row, row, par, par],
                 out_shape=[jax.ShapeDtypeStruct((T, D), F32), jax.ShapeDtypeStruct((T, D), BF16),
                            jax.ShapeDtypeStruct((1, D), F32), jax.ShapeDtypeStruct((1, D), F32)],
                 sem=("arbitrary",))(x, f, g, b, dy)


def _loss_fwd(y, tgt):
    T, D = y.shape
    tm = _tile(T, 256, SUBLANE)

    def body(y_ref, t_ref, l_ref, dy_ref):
        d = y_ref[...] - t_ref[...]
        dy_ref[...] = d * (1.0 / D)

        @pl.when(pl.program_id(0) == 0)
        def _():
            l_ref[...] = jnp.zeros_like(l_ref)

        l_ref[...] += jnp.sum(d * d, axis=0, keepdims=True) * (0.5 / D)

    row = pl.BlockSpec((tm, D), lambda i: (i, 0))
    par = pl.BlockSpec((1, D), lambda i: (0, 0))
    return _call(body, name="loss", grid=(T // tm,), in_specs=[row, row], out_specs=[par, row],
                 out_shape=[jax.ShapeDtypeStruct((1, D), F32), jax.ShapeDtypeStruct((T, D), F32)],
                 sem=("arbitrary",))(y, tgt)


def _swiglu_fn(h1, h3):
    return h1 * jax.nn.sigmoid(h1) * h3


def _swiglu_fwd(h13):
    T, F2 = h13.shape
    F = F2 // 2
    tm = _tile(T, 256, SUBLANE)

    def body(h1_ref, h3_ref, o_ref):
        o_ref[...] = _swiglu_fn(h1_ref[...], h3_ref[...]).astype(BF16)

    return _call(body, name="swiglu_fwd", grid=(T // tm,),
                 in_specs=[pl.BlockSpec((tm, F), lambda i: (i, 0)), pl.BlockSpec((tm, F), lambda i: (i, 1))],
                 out_specs=pl.BlockSpec((tm, F), lambda i: (i, 0)),
                 out_shape=jax.ShapeDtypeStruct((T, F), BF16), sem=("parallel",))(h13, h13)


def _swiglu_bwd(h13, dact):
    T, F2 = h13.shape
    F = F2 // 2
    tm = _tile(T, 256, SUBLANE)

    def body(h1_ref, h3_ref, d_ref, o_ref):
        _, vjp = jax.vjp(_swiglu_fn, h1_ref[...], h3_ref[...])
        d1, d3 = vjp(d_ref[...])
        o_ref[:, :F] = d1.astype(BF16)
        o_ref[:, F:] = d3.astype(BF16)

    return _call(body, name="swiglu_bwd", grid=(T // tm,),
                 in_specs=[pl.BlockSpec((tm, F), lambda i: (i, 0)), pl.BlockSpec((tm, F), lambda i: (i, 1)),
                           pl.BlockSpec((tm, F), lambda i: (i, 0))],
                 out_specs=pl.BlockSpec((tm, F2), lambda i: (i, 0)),
                 out_shape=jax.ShapeDtypeStruct((T, F2), BF16), sem=("parallel",))(h13, h13, dact)


@jax.custom_vjp
def _chunkmix(wm, vn, bs):
    n = vn.shape[0] // CHUNK
    return jnp.concatenate([_dg(wm, vn[c * CHUNK:(c + 1) * CHUNK], 1, 0) + bs for c in range(n)], axis=0)


def _chunkmix_fwd(wm, vn, bs):
    return _chunkmix(wm, vn, bs), (wm, vn)


def _chunkmix_bwd(res, ct):
    wm, vn = res
    n = vn.shape[0] // CHUNK
    cts = [ct[c * CHUNK:(c + 1) * CHUNK] for c in range(n)]
    dvn = jnp.concatenate([_dg(wm, cts[c], 0, 0) for c in range(n)], axis=0)
    dwm = sum(_dg(cts[c], vn[c * CHUNK:(c + 1) * CHUNK], 1, 1) for c in range(n))
    dbs = sum(jnp.sum(cts[c], axis=1, keepdims=True) for c in range(n))
    return dwm, dvn, dbs


_chunkmix.defvjp(_chunkmix_fwd, _chunkmix_bwd)


def _gmlp_fn(u, v, lg, lb, ws, bs):
    ug = _gelu(u)
    vn = _norm(_gelu(v), lg, lb)
    r = lax.broadcasted_iota(jnp.int32, (CHUNK, CHUNK), 0)
    c = lax.broadcasted_iota(jnp.int32, (CHUNK, CHUNK), 1)
    wm = jnp.where(r >= c, ws, 0.0)
    return ug * _chunkmix(wm, vn, bs)


def _gmlp_specs(tm, order):
    def at(col0):
        return pl.BlockSpec((tm, GROUP_DIM), lambda *ids: (order(*ids)[0], col0 + order(*ids)[1]))
    par = pl.BlockSpec((1, GROUP_DIM), lambda *ids: (0, order(*ids)[1]))
    ws = pl.BlockSpec((None, CHUNK, CHUNK), lambda *ids: (order(*ids)[1], 0, 0))
    bs = pl.BlockSpec((None, CHUNK, 1), lambda *ids: (order(*ids)[1], 0, 0))
    return at, par, ws, bs


def _gmlp_fwd(hin, lg, lb, ws, bs):
    T = hin.shape[0]
    tm = _tile(T, 512, CHUNK)
    at, par, wsp, bsp = _gmlp_specs(tm, lambda i, g: (i, g))

    def body(u_ref, v_ref, lg_ref, lb_ref, ws_ref, bs_ref, o_ref):
        o_ref[...] = _gmlp_fn(u_ref[...], v_ref[...], lg_ref[...], lb_ref[...], ws_ref[...],
                              bs_ref[...]).astype(BF16)

    return _call(body, name="gmlp_fwd", grid=(T // tm, A_GROUPS),
                 in_specs=[at(0), at(A_GROUPS), par, par, wsp, bsp], out_specs=at(0),
                 out_shape=jax.ShapeDtypeStruct((T, A_WIDTH), BF16),
                 sem=("parallel", "parallel"))(hin, hin, lg, lb, ws, bs)


def _gmlp_bwd(hin, lg, lb, ws, bs, dyab):
    T = hin.shape[0]
    tm = _tile(T, 512, CHUNK)
    at, par, wsp, bsp = _gmlp_specs(tm, lambda g, i: (i, g))

    def body(u_ref, v_ref, lg_ref, lb_ref, ws_ref, bs_ref, dy_ref,
             du_ref, dv_ref, dlg_ref, dlb_ref, dws_ref, dbs_ref):
        _, vjp = jax.vjp(_gmlp_fn, u_ref[...], v_ref[...], lg_ref[...], lb_ref[...], ws_ref[...], bs_ref[...])
        du, dv, dlg, dlb, dws, dbs = vjp(dy_ref[...])
        du_ref[...] = du.astype(BF16)
        dv_ref[...] = dv.astype(BF16)

        @pl.when(pl.program_id(1) == 0)
        def _():
            dlg_ref[...] = jnp.zeros_like(dlg_ref)
            dlb_ref[...] = jnp.zeros_like(dlb_ref)
            dws_ref[...] = jnp.zeros_like(dws_ref)
            dbs_ref[...] = jnp.zeros_like(dbs_ref)

        dlg_ref[...] += dlg
        dlb_ref[...] += dlb
        dws_ref[...] += dws
        dbs_ref[...] += dbs

    half = jax.ShapeDtypeStruct((T, A_WIDTH), BF16)
    return _call(body, name="gmlp_bwd", grid=(A_GROUPS, T // tm),
                 in_specs=[at(0), at(A_GROUPS), par, par, wsp, bsp, at(0)],
                 out_specs=[at(0), at(0), par, par, wsp, bsp],
                 out_shape=[half, half, jax.ShapeDtypeStruct((1, A_WIDTH), F32),
                            jax.ShapeDtypeStruct((1, A_WIDTH), F32),
                            jax.ShapeDtypeStruct((A_GROUPS, CHUNK, CHUNK), F32),
                            jax.ShapeDtypeStruct((A_GROUPS, CHUNK, 1), F32)],
                 sem=("parallel", "arbitrary"))(hin, hin, lg, lb, ws, bs, dyab)


def _glu_fn(a, gt):
    return a * jax.nn.sigmoid(gt)


def _gn_silu_fn(c, cb, gg, gb):
    y = _norm(c + cb, gg, gb)
    return y * jax.nn.sigmoid(y)


def _conv_taps(w_ref, src_ref, row0, first):
    acc = None
    for k in range(CONV_WIDTH):
        term = w_ref[pl.ds(k, 1), :] * src_ref[pl.ds(row0 + first(k), CONV_ROWS), :]
        acc = term if acc is None else acc + term
    return acc


def _conv_specs(S, order):
    def at(col0):
        return pl.BlockSpec((S, GROUP_DIM), lambda *ids: (order(*ids)[0], col0 + order(*ids)[1]))
    par = pl.BlockSpec((1, GROUP_DIM), lambda *ids: (0, order(*ids)[1]))
    cw = pl.BlockSpec((CONV_WIDTH, GROUP_DIM), lambda *ids: (0, order(*ids)[1]))
    return at, par, cw


def _conv_fwd(hin, cw, cb, gg, gb, n_ex, S):
    T = hin.shape[0]
    at, par, cwp = _conv_specs(S, lambda e, g: (e, g))
    ngrp = B_WIDTH // GROUP_DIM
    lead = CONV_PAD - (CONV_WIDTH - 1)

    def body(a_ref, gt_ref, cw_ref, cb_ref, gg_ref, gb_ref, o_ref, pad_ref):
        pad_ref[0:CONV_PAD, :] = jnp.zeros((CONV_PAD, GROUP_DIM), F32)
        for r in range(S // CONV_ROWS):
            rows = pl.ds(r * CONV_ROWS, CONV_ROWS)
            pad_ref[pl.ds(CONV_PAD + r * CONV_ROWS, CONV_ROWS), :] = _glu_fn(a_ref[rows, :], gt_ref[rows, :])
        for r in range(S // CONV_ROWS):
            c = _conv_taps(cw_ref, pad_ref, r * CONV_ROWS, lambda k: lead + k)
            o_ref[pl.ds(r * CONV_ROWS, CONV_ROWS), :] = _gn_silu_fn(
                c, cb_ref[...], gg_ref[...], gb_ref[...]).astype(BF16)

    return _call(body, name="conv_fwd", grid=(n_ex, ngrp),
                 in_specs=[at(2 * A_GROUPS), at(2 * A_GROUPS + ngrp), cwp, par, par, par], out_specs=at(0),
                 out_shape=jax.ShapeDtypeStruct((T, B_WIDTH), BF16),
                 scratch=[pltpu.VMEM((S + CONV_PAD, GROUP_DIM), F32)],
                 sem=("parallel", "parallel"))(hin, hin, cw, cb, gg, gb)


def _conv_bwd(hin, cw, cb, gg, gb, dyab, n_ex, S):
    T = hin.shape[0]
    at, par, cwp = _conv_specs(S, lambda g, e: (e, g))
    ngrp = B_WIDTH // GROUP_DIM
    lead = CONV_PAD - (CONV_WIDTH - 1)
    nchunk = S // CONV_ROWS

    def body(a_ref, gt_ref, cw_ref, cb_ref, gg_ref, gb_ref, dy_ref,
             da_ref, dgt_ref, dcw_ref, dcb_ref, dgg_ref, dgb_ref, pad_ref, dcp_ref, wacc_ref, pacc_ref):
        @pl.when(pl.program_id(1) == 0)
        def _():
            wacc_ref[...] = jnp.zeros_like(wacc_ref)
            pacc_ref[...] = jnp.zeros_like(pacc_ref)

        pad_ref[0:CONV_PAD, :] = jnp.zeros((CONV_PAD, GROUP_DIM), F32)
        dcp_ref[S:S + CONV_PAD, :] = jnp.zeros((CONV_PAD, GROUP_DIM), F32)
        for r in range(nchunk):
            rows = pl.ds(r * CONV_ROWS, CONV_ROWS)
            pad_ref[pl.ds(CONV_PAD + r * CONV_ROWS, CONV_ROWS), :] = _glu_fn(a_ref[rows, :], gt_ref[rows, :])
        for r in range(nchunk):
            rows = pl.ds(r * CONV_ROWS, CONV_ROWS)
            c = _conv_taps(cw_ref, pad_ref, r * CONV_ROWS, lambda k: lead + k)
            _, vjp = jax.vjp(_gn_silu_fn, c, cb_ref[...], gg_ref[...], gb_ref[...])
            dc, dcb, dgg, dgb = vjp(dy_ref[rows, :])
            dcp_ref[rows, :] = dc
            pacc_ref[0:1, :] += dcb
            pacc_ref[1:2, :] += dgg
            pacc_ref[2:3, :] += dgb
        for r in range(nchunk):
            rows = pl.ds(r * CONV_ROWS, CONV_ROWS)
            dh = _conv_taps(cw_ref, dcp_ref, r * CONV_ROWS, lambda k: CONV_WIDTH - 1 - k)
            _, vjp = jax.vjp(_glu_fn, a_ref[rows, :], gt_ref[rows, :])
            da, dgt = vjp(dh)
            da_ref[rows, :] = da.astype(BF16)
            dgt_ref[rows, :] = dgt.astype(BF16)
            dc = dcp_ref[rows, :]
            for k in range(CONV_WIDTH):
                prod = dc * pad_ref[pl.ds(r * CONV_ROWS + lead + k, CONV_ROWS), :]
                wacc_ref[k] += jnp.sum(prod.reshape(CONV_ROWS // SUBLANE, SUBLANE, GROUP_DIM), axis=0)
        for k in range(CONV_WIDTH):
            dcw_ref[pl.ds(k, 1), :] = jnp.sum(wacc_ref[k], axis=0, keepdims=True)
        dcb_ref[...] = pacc_ref[0:1, :]
        dgg_ref[...] = pacc_ref[1:2, :]
        dgb_ref[...] = pacc_ref[2:3, :]

    half = jax.ShapeDtypeStruct((T, B_WIDTH), BF16)
    vec = jax.ShapeDtypeStruct((1, B_WIDTH), F32)
    return _call(body, name="conv_bwd", grid=(ngrp, n_ex),
                 in_specs=[at(2 * A_GROUPS), at(2 * A_GROUPS + ngrp), cwp, par, par, par, at(ngrp)],
                 out_specs=[at(0), at(0), cwp, par, par, par],
                 out_shape=[half, half, jax.ShapeDtypeStruct((CONV_WIDTH, B_WIDTH), F32), vec, vec, vec],
                 scratch=[pltpu.VMEM((S + CONV_PAD, GROUP_DIM), F32), pltpu.VMEM((S + CONV_PAD, GROUP_DIM), F32),
                          pltpu.VMEM((CONV_WIDTH, SUBLANE, GROUP_DIM), F32), pltpu.VMEM((SUBLANE, GROUP_DIM), F32)],
                 sem=("parallel", "arbitrary"))(hin, hin, cw, cb, gg, gb, dyab)


def _sb_consts():
    r = lax.broadcasted_iota(jnp.int32, (SB_TK + SUBLANE, 2 * SB_TK), 0)
    c = lax.broadcasted_iota(jnp.int32, (SB_TK + SUBLANE, 2 * SB_TK), 1) % SB_TK
    tail = r >= SB_TK
    u_after = jnp.where((c > r) | tail, 1.0, 0.0).astype(BF16)
    u_before = jnp.where((c < r) | tail, 1.0, 0.0).astype(BF16)
    s = lax.broadcasted_iota(jnp.int32, (SB_TK, SB_TQ), 0)
    t = lax.broadcasted_iota(jnp.int32, (SB_TK, SB_TQ), 1)
    masks = [(s + SB_TK * d) < t for d in range(SB_TQ // SB_TK)]
    return u_after, u_before, masks


def _sb_store_split(hl, h, j, x):
    hi = x.astype(BF16)
    hl[h, j, :SB_TK] = hi
    hl[h, j, SB_TK:] = (x - hi.astype(F32)).astype(BF16)


def _split_sum(u, x):
    hi = x.astype(BF16)
    lo = (x - hi.astype(F32)).astype(BF16)
    res = _dg(u, jnp.concatenate([hi, lo], axis=0), 1, 0)
    return res[:SB_TK], res[SB_TK:]


def _sb_sums(u, hl, ws, tot, h, j):
    res = _dg(u, hl[h, j], 1, 0)
    ws[h, j] = res[:SB_TK]
    tot[h, j] = res[SB_TK:]


def _sb_tiles(trips, per, fn, carry=None, descending=False):
    def trip(t, c):
        t = trips - 1 - t if descending else t
        for u in (reversed(range(per)) if descending else range(per)):
            if carry is None:
                fn(per * t + u)
            else:
                c = fn(per * t + u, c)
        return c

    return lax.fori_loop(0, trips, trip, 0 if carry is None else carry)


def _add8(x, row8):
    return (x.reshape(-1, SUBLANE, x.shape[-1]) + row8[None]).reshape(x.shape)


def _sb_blocked(dst, src, h, n, width):
    for t in range(n):
        dst[h, t] = src[h, :, t * width:(t + 1) * width].astype(dst.dtype)


def _sb_spec(S, nh):
    def at(part):
        return pl.BlockSpec((nh, C_HEAD_DIM, S), lambda e, h: (part * (C_HEADS // nh) + h, 0, e))
    return at


def _sb_fwd(qkvT, n_ex, S):
    T = qkvT.shape[-1]
    nq, nk = S // SB_TQ, S // SB_TK
    per = SB_TQ // SB_TK
    heads = range(SB_HEADS)

    def body(q_ref, k_ref, v_ref, o_ref, qb, kb, vb, ob, zs, hl, ws, tot, ab):
        for h in heads:
            _sb_blocked(qb, q_ref, h, nq, SB_TQ)
            _sb_blocked(kb, k_ref, h, nk, SB_TK)
            _sb_blocked(vb, v_ref, h, nk, SB_TK)
        u_after, _, masks = _sb_consts()

        def soft(h, j, mask):
            z = zs[h, j]
            sp = jnp.maximum(z, 0.0) + jnp.log1p(jnp.exp(-jnp.abs(z)))
            _sb_store_split(hl, h, j, -sp if mask is None else jnp.where(mask, -sp, 0.0))
            zs[h, j] = z - sp

        def weigh(h, j, carry, mask):
            att = jnp.exp(zs[h, j] + _add8(ws[h, j], carry))
            if mask is not None:
                att = jnp.where(mask, att, 0.0)
            ab[h, j] = att.astype(_MXU)
            return carry + tot[h, j]

        def qtile(i, _):
            qs = [qb[h, i] for h in heads]

            def scores(j):
                for h in heads:
                    zs[h, j] = _dg(kb[h, j], qs[h], 0, 0) * SB_SCALE

            _sb_tiles(i + 1, per, scores)
            _sb_tiles(i, per, lambda j: [soft(h, j, None) for h in heads])
            for d in range(per):
                for h in heads:
                    soft(h, per * i + d, masks[d])
            _sb_tiles(i + 1, per, lambda j: [_sb_sums(u_after, hl, ws, tot, h, j) for h in heads])
            carry = tuple(jnp.zeros((SUBLANE, SB_TQ), F32) for h in heads)
            for d in reversed(range(per)):
                carry = tuple(weigh(h, per * i + d, carry[h], masks[d]) for h in heads)
            _sb_tiles(i, per, lambda j, c: tuple(weigh(h, j, c[h], None) for h in heads), carry, descending=True)
            acc = tuple(jnp.zeros((C_HEAD_DIM, SB_TQ), F32) for h in heads)
            acc = _sb_tiles(i + 1, per, lambda j, a: tuple(a[h] + _dg(vb[h, j], ab[h, j], 1, 0) for h in heads), acc)
            for h in heads:
                ob[h, i] = acc[h]
            return 0

        lax.fori_loop(0, nq, qtile, 0)
        for h in heads:
            for i in range(nq):
                o_ref[h, :, i * SB_TQ:(i + 1) * SB_TQ] = ob[h, i].astype(BF16)

    at = _sb_spec(S, SB_HEADS)
    qshape, kshape = (SB_HEADS, nq, C_HEAD_DIM, SB_TQ), (SB_HEADS, nk, C_HEAD_DIM, SB_TK)
    tiles = (SB_HEADS, nk, SB_TK, SB_TQ)
    return _call(body, name="sb_fwd", grid=(n_ex, C_HEADS // SB_HEADS), in_specs=[at(0), at(1), at(2)],
                 out_specs=at(0), out_shape=jax.ShapeDtypeStruct((C_HEADS, C_HEAD_DIM, T), BF16),
                 scratch=[pltpu.VMEM(qshape, _MXU), pltpu.VMEM(kshape, _MXU), pltpu.VMEM(kshape, _MXU),
                          pltpu.VMEM(qshape, F32), pltpu.VMEM(tiles, F32),
                          pltpu.VMEM((SB_HEADS, nk, 2 * SB_TK, SB_TQ), BF16), pltpu.VMEM(tiles, F32),
                          pltpu.VMEM((SB_HEADS, nk, SUBLANE, SB_TQ), F32), pltpu.VMEM(tiles, _MXU)],
                 sem=("parallel", "parallel"))(qkvT, qkvT, qkvT)


def _sb_bwd(qkvT, doT, n_ex, S):
    T = qkvT.shape[-1]
    nq, nk = S // SB_TQ, S // SB_TK
    per = SB_TQ // SB_TK
    heads = range(SB_BWD_HEADS)

    def body(q_ref, k_ref, v_ref, do_ref, dq_ref, dk_ref, dv_ref, qb, kb, vb, dob, dqa, dka, dva, dl_s, sg_s):
        for h in heads:
            _sb_blocked(qb, q_ref, h, nq, SB_TQ)
            _sb_blocked(dob, do_ref, h, nq, SB_TQ)
            _sb_blocked(kb, k_ref, h, nk, SB_TK)
            _sb_blocked(vb, v_ref, h, nk, SB_TK)
        dka[...] = jnp.zeros_like(dka)
        dva[...] = jnp.zeros_like(dva)
        u_after, u_before, masks = _sb_consts()

        def rebuild(h, j, qi, gi, carry, mask):
            z = _dg(kb[h, j], qi, 0, 0) * SB_SCALE
            e = jnp.exp(-jnp.abs(z))
            sp = jnp.maximum(z, 0.0) + jnp.log1p(e)
            within, total = _split_sum(u_after, -sp if mask is None else jnp.where(mask, -sp, 0.0))
            att = jnp.exp((z - sp) + _add8(within, carry))
            r = 1.0 / (1.0 + e)
            sig = jnp.where(z >= 0.0, r, e * r)
            if mask is not None:
                att = jnp.where(mask, att, 0.0)
                sig = jnp.where(mask, sig, 0.0)
            dl_s[h, j] = _dg(vb[h, j], gi, 0, 0) * att
            sg_s[h, j] = sig
            dva[h, j] += _dg(gi, att, 1, 1)
            return carry + total

        def push(h, j, qi, carry, dq):
            dlogit, sig = dl_s[h, j], sg_s[h, j]
            within, total = _split_sum(u_before, dlogit)
            dz = (dlogit * (1.0 - sig) - sig * _add8(within, carry)) * SB_SCALE
            dka[h, j] += _dg(qi, dz, 1, 1)
            return carry + total, dq + _dg(kb[h, j], dz, 1, 0)

        def qtile(i, _):
            qs = [qb[h, i] for h in heads]
            gs = [dob[h, i] for h in heads]
            carry = tuple(jnp.zeros((SUBLANE, SB_TQ), F32) for h in heads)
            for d in reversed(range(per)):
                carry = tuple(rebuild(h, per * i + d, qs[h], gs[h], carry[h], masks[d]) for h in heads)
            _sb_tiles(i, per, lambda j, c: tuple(rebuild(h, j, qs[h], gs[h], c[h], None) for h in heads), carry,
                      descending=True)
            st = tuple((jnp.zeros((SUBLANE, SB_TQ), F32), jnp.zeros((C_HEAD_DIM, SB_TQ), F32)) for h in heads)
            st = _sb_tiles(i + 1, per, lambda j, s: tuple(push(h, j, qs[h], *s[h]) for h in heads), st)
            for h in heads:
                dqa[h, i] = st[h][1]
            return 0

        lax.fori_loop(0, nq, qtile, 0)
        for h in heads:
            for i in range(nq):
                dq_ref[h, :, i * SB_TQ:(i + 1) * SB_TQ] = dqa[h, i].astype(BF16)
            for j in range(nk):
                dk_ref[h, :, j * SB_TK:(j + 1) * SB_TK] = dka[h, j].astype(BF16)
                dv_ref[h, :, j * SB_TK:(j + 1) * SB_TK] = dva[h, j].astype(BF16)

    nh = SB_BWD_HEADS
    at = _sb_spec(S, nh)
    out = jax.ShapeDtypeStruct((C_HEADS, C_HEAD_DIM, T), BF16)
    qshape, kshape = (nh, nq, C_HEAD_DIM, SB_TQ), (nh, nk, C_HEAD_DIM, SB_TK)
    tiles = (nh, nk, SB_TK, SB_TQ)
    return _call(body, name="sb_bwd", grid=(n_ex, C_HEADS // nh), in_specs=[at(0), at(1), at(2), at(0)],
                 out_specs=[at(0), at(0), at(0)], out_shape=[out, out, out],
                 scratch=[pltpu.VMEM(qshape, _MXU), pltpu.VMEM(kshape, _MXU), pltpu.VMEM(kshape, _MXU),
                          pltpu.VMEM(qshape, _MXU), pltpu.VMEM(qshape, F32), pltpu.VMEM(kshape, F32),
                          pltpu.VMEM(kshape, F32), pltpu.VMEM(tiles, F32), pltpu.VMEM(tiles, F32)],
                 sem=("parallel", "parallel"))(qkvT, qkvT, qkvT, doT)


def _xattn_fn(q, k, v):
    s = _dot_t(q, k) * MEM_SCALE
    m = lax.stop_gradient(jnp.max(s, axis=-1, keepdims=True))
    p = jnp.exp(s - m)
    p = p / jnp.sum(p, axis=-1, keepdims=True)
    return _dot(p, v)


def _xattn_fwd(q, kk, vv, n_ex, S):
    T = q.shape[0]
    tq = _tile(S, 1024, SUBLANE)
    nqt = S // tq

    def body(q_ref, k_ref, v_ref, o_ref):
        o_ref[...] = _xattn_fn(q_ref[...], k_ref[...], v_ref[...]).astype(BF16)

    qs = pl.BlockSpec((tq, MEM_HEAD_DIM), lambda e, i, h: (e * nqt + i, h))
    ks = pl.BlockSpec((MEM_LEN, MEM_HEAD_DIM), lambda e, i, h: (e, h))
    return _call(body, name="xattn_fwd", grid=(n_ex, nqt, MEM_HEADS), in_specs=[qs, ks, ks], out_specs=qs,
                 out_shape=jax.ShapeDtypeStruct((T, D_MODEL), BF16),
                 sem=("parallel", "parallel", "parallel"))(q, kk, vv)


def _xattn_bwd(q, kk, vv, do, n_ex, S):
    T = q.shape[0]
    tq = _tile(S, 1024, SUBLANE)
    nqt = S // tq

    def body(q_ref, k_ref, v_ref, do_ref, dq_ref, dk_ref, dv_ref):
        _, vjp = jax.vjp(_xattn_fn, q_ref[...].astype(F32), k_ref[...].astype(F32), v_ref[...].astype(F32))
        dq, dk, dv = vjp(do_ref[...].astype(F32))
        dq_ref[...] = dq.astype(BF16)

        @pl.when(pl.program_id(2) == 0)
        def _():
            dk_ref[...] = jnp.zeros_like(dk_ref)
            dv_ref[...] = jnp.zeros_like(dv_ref)

        dk_ref[...] += dk
        dv_ref[...] += dv

    qs = pl.BlockSpec((tq, MEM_HEAD_DIM), lambda e, h, i: (e * nqt + i, h))
    ks = pl.BlockSpec((MEM_LEN, MEM_HEAD_DIM), lambda e, h, i: (e, h))
    kv = jax.ShapeDtypeStruct(kk.shape, F32)
    return _call(body, name="xattn_bwd", grid=(n_ex, MEM_HEADS, nqt), in_specs=[qs, ks, ks, qs],
                 out_specs=[qs, ks, ks], out_shape=[jax.ShapeDtypeStruct((T, D_MODEL), BF16), kv, kv],
                 sem=("parallel", "parallel", "arbitrary"))(q, kk, vv, do)


def _local_step(x, mem, tgt, W):
    n_ex, S, D = x.shape
    T = n_ex * S
    h = x.reshape(T, D)
    mem2 = mem.reshape(n_ex * MEM_LEN, D)
    row = lambda v: v.reshape(1, -1)
    saved = []
    for l in range(DEPTH):
        sv = dict(x0=h)
        if l % 2 == 0:
            e = l // 2
            hin = _matmul(h, W["w_in_ab"], lb=e, name="in_proj")
            ya = _gmlp_fwd(hin, row(W["gmlp_ln_g"][e]), row(W["gmlp_ln_b"][e]), W["gmlp_w_s"][e],
                           W["gmlp_b_s"][e][:, :, None])
            yb = _conv_fwd(hin, W["conv_w"][e], row(W["conv_b"][e]), row(W["conv_gn_g"][e]),
                           row(W["conv_gn_b"][e]), n_ex, S)
            yab = jnp.concatenate([ya, yb], axis=1)
            mix = _matmul(yab, W["w_out_ab"], lb=e, name="out_proj")
            sv.update(hin=hin, yab=yab)
        else:
            o = l // 2
            qkvT = _matmul(W["w_qkv_cT"], h, mode="nt", la=o, out_dtype=BF16, name="qkv_proj")
            yT = _sb_fwd(qkvT.reshape(3 * C_HEADS, C_HEAD_DIM, T), n_ex, S)
            mix = _matmul(yT.reshape(D, T), W["w_out_c"], mode="tn", lb=o, name="sb_out_proj")
            sv.update(qkvT=qkvT, yT=yT)
        x1 = _ln_fwd(h, mix, row(W["ln_g"][l, 0]), row(W["ln_b"][l, 0]))
        q = _matmul(x1, W["mem_wq"], lb=l, out_dtype=BF16, name="mem_q")
        kk = _matmul(mem2, W["mem_wk"], lb=l, out_dtype=BF16, name="mem_kv")
        vv = _matmul(mem2, W["mem_wv"], lb=l, out_dtype=BF16, name="mem_kv")
        oc = _xattn_fwd(q, kk, vv, n_ex, S)
        cross = _matmul(oc, W["mem_wo"], lb=l, name="mem_o")
        x2 = _ln_fwd(x1, cross, row(W["ln_g"][l, 1]), row(W["ln_b"][l, 1]))
        h13 = _matmul(x2, W["ffn_w13"], lb=l, name="ffn_in")
        act = _swiglu_fwd(h13)
        f = _matmul(act, W["ffn_w2"], lb=l, name="ffn_out")
        x3 = _ln_fwd(x2, f, row(W["ln_g"][l, 2]), row(W["ln_b"][l, 2]))
        sv.update(mix=mix, x1=x1, q=q, kk=kk, vv=vv, oc=oc, cross=cross, x2=x2, h13=h13, act=act, f=f)
        saved.append(sv)
        h = x3

    loss_part, dh = _loss_fwd(h, tgt.reshape(T, D))

    G = {k: [None] * n for k, n in [
        ("w_in_ab", 2), ("gmlp_ln_g", 2), ("gmlp_ln_b", 2), ("gmlp_w_s", 2), ("gmlp_b_s", 2), ("conv_w", 2),
        ("conv_b", 2), ("conv_gn_g", 2), ("conv_gn_b", 2), ("w_out_ab", 2), ("w_qkv_c", 2), ("w_out_c", 2),
        ("mem_wq", 4), ("mem_wk", 4), ("mem_wv", 4), ("mem_wo", 4), ("ffn_w1", 4), ("ffn_w3", 4),
        ("ffn_w2", 4), ("ln_g", 4), ("ln_b", 4)]}
    for l in reversed(range(DEPTH)):
        sv = saved[l]
        lng, lnb = [None] * 3, [None] * 3
        dx2, df, lng[2], lnb[2] = _ln_bwd(sv["x2"], sv["f"], row(W["ln_g"][l, 2]), row(W["ln_b"][l, 2]), dh)
        dact = _matmul(df, W["ffn_w2"], mode="nt", lb=l, name="ffn_out_dx")
        G["ffn_w2"][l] = _matmul(sv["act"], df, mode="tn", name="ffn_out_dw")
        dh13 = _swiglu_bwd(sv["h13"], dact)
        dx2 = _matmul(dh13, W["ffn_w13"], mode="nt", lb=l, add=dx2, name="ffn_in_dx")
        dw13 = _matmul(sv["x2"], dh13, mode="tn", name="ffn_in_dw")
        ff = dw13.shape[1] // 2
        G["ffn_w1"][l], G["ffn_w3"][l] = dw13[:, :ff], dw13[:, ff:]
        dx1, dcross, lng[1], lnb[1] = _ln_bwd(sv["x1"], sv["cross"], row(W["ln_g"][l, 1]),
                                              row(W["ln_b"][l, 1]), dx2)
        doc = _matmul(dcross, W["mem_wo"], mode="nt", lb=l, out_dtype=BF16, name="mem_o_dx")
        G["mem_wo"][l] = _matmul(sv["oc"], dcross, mode="tn", name="mem_o_dw")
        dq, dkk, dvv = _xattn_bwd(sv["q"], sv["kk"], sv["vv"], doc, n_ex, S)
        dx1 = _matmul(dq, W["mem_wq"], mode="nt", lb=l, add=dx1, name="mem_q_dx")
        G["mem_wq"][l] = _matmul(sv["x1"], dq, mode="tn", name="mem_q_dw")
        G["mem_wk"][l] = _matmul(mem2, dkk, mode="tn", name="mem_kv_dw")
        G["mem_wv"][l] = _matmul(mem2, dvv, mode="tn", name="mem_kv_dw")
        dx0, dmix, lng[0], lnb[0] = _ln_bwd(sv["x0"], sv["mix"], row(W["ln_g"][l, 0]), row(W["ln_b"][l, 0]), dx1)
        if l % 2 == 0:
            e = l // 2
            dyab = _matmul(dmix, W["w_out_ab"], mode="nt", lb=e, name="out_proj_dx")
            G["w_out_ab"][e] = _matmul(sv["yab"], dmix, mode="tn", name="out_proj_dw")
            du, dv, dlg, dlb, dws, dbs = _gmlp_bwd(
                sv["hin"], row(W["gmlp_ln_g"][e]), row(W["gmlp_ln_b"][e]), W["gmlp_w_s"][e],
                W["gmlp_b_s"][e][:, :, None], dyab)
            da, dgt, dcw, dcb, dgg, dgb = _conv_bwd(
                sv["hin"], W["conv_w"][e], row(W["conv_b"][e]), row(W["conv_gn_g"][e]),
                row(W["conv_gn_b"][e]), dyab, n_ex, S)
            dhin = jnp.concatenate([du, dv, da, dgt], axis=1)
            dh = _matmul(dhin, W["w_in_ab"], mode="nt", lb=e, add=dx0, name="in_proj_dx")
            G["w_in_ab"][e] = _matmul(sv["x0"], dhin, mode="tn", name="in_proj_dw")
            G["gmlp_ln_g"][e], G["gmlp_ln_b"][e] = dlg[0], dlb[0]
            G["gmlp_w_s"][e], G["gmlp_b_s"][e] = dws, dbs[:, :, 0]
            G["conv_w"][e], G["conv_b"][e], G["conv_gn_g"][e], G["conv_gn_b"][e] = dcw, dcb[0], dgg[0], dgb[0]
        else:
            o = l // 2
            dyT = _matmul(W["w_out_c"], dmix, mode="nt", la=o, out_dtype=BF16, name="sb_out_proj_dx")
            G["w_out_c"][o] = _matmul(sv["yT"].reshape(D, T), dmix, name="sb_out_proj_dw")
            dqT, dkT, dvT = _sb_bwd(sv["qkvT"].reshape(3 * C_HEADS, C_HEAD_DIM, T),
                                    dyT.reshape(C_HEADS, C_HEAD_DIM, T), n_ex, S)
            dqkvT = jnp.concatenate([dqT, dkT, dvT], axis=0).reshape(3 * D, T)
            dh = _matmul(dqkvT, W["w_qkv_cT"], mode="tn", lb=o, add=dx0, name="qkv_proj_dx")
            G["w_qkv_c"][o] = _matmul(dqkvT, sv["x0"], name="qkv_proj_dw").T
        G["ln_g"][l] = jnp.concatenate(lng, axis=0)
        G["ln_b"][l] = jnp.concatenate(lnb, axis=0)
    grads = {k: jnp.stack(v, axis=0) for k, v in G.items()}
    return loss_part, dh.reshape(n_ex, S, D), grads


def _allgather(p, name):
    def body(p_ref, out_ref, send_sems, recv_sems, local_sem):
        x, y, c = (lax.axis_index(a) for a in AXES)
        me, sibling = (x, y, c), (x, y, 1 - c)
        chips = [(1 - x, y), (x, 1 - y), (1 - x, 1 - y)]

        def rows(px, py, pc):
            return out_ref.at[4 * px + 2 * py + pc]

        def copy(k, block, to, src=None):
            return pltpu.make_async_remote_copy(
                src_ref=rows(*block) if src is None else src, dst_ref=rows(*block), send_sem=send_sems.at[k],
                recv_sem=recv_sems.at[k], device_id=to, device_id_type=pl.DeviceIdType.MESH)

        mine = pltpu.make_async_copy(p_ref, rows(*me), local_sem)
        mine.start()
        first = [copy(0, me, sibling, src=p_ref)]
        first += [copy(1 + j, me, (*chip, c), src=p_ref) for j, chip in enumerate(chips)]
        for cp in first:
            cp.start()
        passed = [copy(4 + j, (*chip, c), sibling) for j, chip in enumerate(chips)]
        for j, chip in enumerate(chips):
            copy(1 + j, (*chip, c), me).wait_recv()
            passed[j].start()
        copy(0, sibling, me).wait_recv()
        for j, chip in enumerate(chips):
            copy(4 + j, (*chip, 1 - c), me).wait_recv()
        for cp in first + passed:
            cp.wait_send()
        mine.wait()

    anyspace = pl.BlockSpec(memory_space=pl.ANY)
    return pl.pallas_call(
        body, name=name, in_specs=[anyspace], out_specs=anyspace,
        out_shape=jax.ShapeDtypeStruct((N_DEV,) + p.shape, p.dtype),
        scratch_shapes=[pltpu.SemaphoreType.DMA((N_DEV - 1,)), pltpu.SemaphoreType.DMA((N_DEV - 1,)),
                        pltpu.SemaphoreType.DMA],
    )(p)


N_CHIP = 4
N_CORE = 2


def _pair_swap(p, name):
    def body(p_ref, out_ref, send_sem, recv_sem):
        x, y, c = (lax.axis_index(a) for a in AXES)
        cp = pltpu.make_async_remote_copy(src_ref=p_ref.at[1 - c], dst_ref=out_ref, send_sem=send_sem,
                                          recv_sem=recv_sem, device_id=(x, y, 1 - c),
                                          device_id_type=pl.DeviceIdType.MESH)
        cp.start()
        cp.wait()

    anyspace = pl.BlockSpec(memory_space=pl.ANY)
    return pl.pallas_call(
        body, name=name, in_specs=[anyspace], out_specs=anyspace,
        out_shape=jax.ShapeDtypeStruct(p.shape[1:], p.dtype),
        scratch_shapes=[pltpu.SemaphoreType.DMA, pltpu.SemaphoreType.DMA],
    )(p)


def _pair_add(p, s):
    _, n, R, C = p.shape
    tr = _tile(R, 512, SUBLANE)

    def body(p_ref, s_ref, o_ref):
        o_ref[...] = p_ref[lax.axis_index("c")] + s_ref[...]

    return _call(body, name="pair_add", grid=(n, R // tr),
                 in_specs=[pl.BlockSpec((N_CORE, None, tr, C), lambda k, i: (0, k, i, 0)),
                           pl.BlockSpec((None, tr, C), lambda k, i: (k, i, 0))],
                 out_specs=pl.BlockSpec((None, tr, C), lambda k, i: (k, i, 0)),
                 out_shape=jax.ShapeDtypeStruct(s.shape, s.dtype), sem=("parallel", "parallel"))(p, s)


def _chip_exchange(q, name):
    def body(q_ref, out_ref, send_sems, recv_sems, local_sem):
        x, y, c = (lax.axis_index(a) for a in AXES)
        mine = 2 * x + y
        local = pltpu.make_async_copy(q_ref.at[mine], out_ref.at[mine], local_sem)
        local.start()
        sends, recvs = [], []
        for j in range(1, N_CHIP):
            px = 1 - x if (j >> 1) & 1 else x
            py = 1 - y if j & 1 else y
            theirs = 2 * px + py
            common = dict(send_sem=send_sems.at[j - 1], recv_sem=recv_sems.at[j - 1], device_id=(px, py, c),
                          device_id_type=pl.DeviceIdType.MESH)
            sends.append(pltpu.make_async_remote_copy(src_ref=q_ref.at[theirs], dst_ref=out_ref.at[mine], **common))
            recvs.append(pltpu.make_async_remote_copy(src_ref=q_ref.at[theirs], dst_ref=out_ref.at[theirs], **common))
        for cp in sends:
            cp.start()
        for cp in recvs:
            cp.wait_recv()
        for cp in sends:
            cp.wait_send()
        local.wait()

    anyspace = pl.BlockSpec(memory_space=pl.ANY)
    return pl.pallas_call(
        body, name=name, in_specs=[anyspace], out_specs=anyspace,
        out_shape=jax.ShapeDtypeStruct(q.shape, q.dtype),
        scratch_shapes=[pltpu.SemaphoreType.DMA((N_CHIP - 1,)), pltpu.SemaphoreType.DMA((N_CHIP - 1,)),
                        pltpu.SemaphoreType.DMA],
    )(q)


def _adamw(parts, w, m, v):
    R, C = w.shape
    n_parts = parts.shape[0]
    tr = _tile(R, 128, SUBLANE)
    c1 = 1.0 - ADAM_B1 ** ADAM_STEP
    c2 = 1.0 - ADAM_B2 ** ADAM_STEP

    def body(p_ref, w_ref, m_ref, v_ref, g_ref, d_ref, mo_ref, vo_ref):
        g = p_ref[0]
        for s in range(1, n_parts):
            g = g + p_ref[s]
        mn = ADAM_B1 * m_ref[...] + (1.0 - ADAM_B1) * g
        vn = ADAM_B2 * v_ref[...] + (1.0 - ADAM_B2) * (g * g)
        m_hat = mn / c1
        v_hat = vn / c2
        g_ref[...] = g
        d_ref[...] = -ADAM_LR * (m_hat / (jnp.sqrt(v_hat) + ADAM_EPS) + ADAM_WD * w_ref[...])
        mo_ref[...] = mn
        vo_ref[...] = vn

    row = pl.BlockSpec((tr, C), lambda i: (i, 0))
    out = jax.ShapeDtypeStruct((R, C), F32)
    return _call(body, name="adamw", grid=(R // tr,),
                 in_specs=[pl.BlockSpec((n_parts, tr, C), lambda i: (0, i, 0)), row, row, row],
                 out_specs=[row, row, row, row], out_shape=[out, out, out, out], sem=("parallel",))(parts, w, m, v)


_MATMUL_W = [("w_in_ab", 2), ("w_out_ab", 1), ("w_qkv_c", 2), ("w_out_c", 1), ("mem_wq", 1), ("mem_wk", 1),
             ("mem_wv", 1), ("mem_wo", 1), ("ffn_w1", 2), ("ffn_w3", 2), ("ffn_w2", 1)]
_SMALL_SHARDED = [("conv_w", 2), ("ln_g", 2), ("ln_b", 2)]
_REPLICATED = ["gmlp_ln_g", "gmlp_ln_b", "gmlp_w_s", "gmlp_b_s", "conv_b", "conv_gn_g", "conv_gn_b"]
_WEIGHTS = ["w_in_ab", "gmlp_ln_g", "gmlp_ln_b", "gmlp_w_s", "gmlp_b_s", "conv_w", "conv_b", "conv_gn_g",
            "conv_gn_b", "w_out_ab", "w_qkv_c", "w_out_c", "mem_wq", "mem_wk", "mem_wv", "mem_wo", "ffn_w1",
            "ffn_w3", "ffn_w2", "ln_g", "ln_b"]
_PACK_ORDER = [n for n, _ in _MATMUL_W] + [n for n, _ in _SMALL_SHARDED] + _REPLICATED
_PACK_ROW_MULT = 128


_FF_AXIS = {"ffn_w1": 2, "ffn_w3": 2, "ffn_w2": 1}


def _to_padded(name, a):
    axis = _FF_AXIS.get(name)
    if axis is None:
        return a
    widths = [(0, 0)] * a.ndim
    widths[axis] = (0, D_FF_SHARD_PAD - a.shape[axis])
    return jnp.pad(a, widths)


def _from_padded(name, a):
    axis = _FF_AXIS.get(name)
    return a if axis is None else lax.slice_in_dim(a, 0, D_FF_SHARD, axis=axis)


def _unshard(g, axis):
    full = jnp.moveaxis(g, 0, axis)
    shp = full.shape
    return full.reshape(shp[:axis] + (shp[axis] * shp[axis + 1],) + shp[axis + 2:])


def _split_shards(full, axis):
    shp = full.shape
    parts = full.reshape(shp[:axis] + (N_DEV, shp[axis] // N_DEV) + shp[axis + 1:])
    return jnp.moveaxis(parts, axis, 0)


def _pack_rows(flat_parts, lead=()):
    flat = jnp.concatenate(flat_parts, axis=-1)
    n = flat.shape[-1]
    per = PACK_COLS * _PACK_ROW_MULT
    total = -(-n // per) * per
    if total != n:
        flat = jnp.concatenate([flat, jnp.zeros(lead + (total - n,), flat.dtype)], axis=-1)
    return flat.reshape(lead + (total // PACK_COLS, PACK_COLS))


def _gather_weights(w):
    big = _pack_rows([w[n].astype(BF16).reshape(-1) for n, _ in _MATMUL_W])
    small = _pack_rows([w[n].reshape(-1) for n, _ in _SMALL_SHARDED])
    big_all = _allgather(big, "gather_matmul_weights").reshape(N_DEV, -1)
    small_all = _allgather(small, "gather_small_weights").reshape(N_DEV, -1)
    full = {n: w[n] for n in _REPLICATED}
    for table, src in ((_MATMUL_W, big_all), (_SMALL_SHARDED, small_all)):
        off = 0
        for n, axis in table:
            size = w[n].size
            full[n] = _unshard(src[:, off:off + size].reshape((N_DEV,) + w[n].shape), axis)
            off += size
    full["w_qkv_cT"] = jnp.swapaxes(full.pop("w_qkv_c"), 1, 2)
    full["ffn_w13"] = jnp.concatenate([full.pop("ffn_w1"), full.pop("ffn_w3")], axis=2)
    return full


def _pack_grads(grads):
    def by_core(p):
        return jnp.swapaxes(p.reshape(N_CHIP, N_CORE, -1), 0, 1).reshape(N_DEV, -1)

    pieces = [by_core(_split_shards(grads[n], axis).reshape(N_DEV, -1)) for n, axis in _MATMUL_W + _SMALL_SHARDED]
    pieces += [jnp.broadcast_to(grads[n].reshape(1, -1), (N_DEV, grads[n].size)) for n in _REPLICATED]
    packed = _pack_rows(pieces, lead=(N_DEV,))
    return packed.reshape((N_CORE, N_CHIP) + packed.shape[1:])


def _pack_local(w):
    return _pack_rows([w[n].reshape(-1) for n in _PACK_ORDER])


def _unpack_local(packed, like):
    flat = packed.reshape(-1)
    out, off = {}, 0
    for n in _PACK_ORDER:
        out[n] = flat[off:off + like[n].size].reshape(like[n].shape)
        off += like[n].size
    return out


def kernel(x, mem, w_in_ab, gmlp_ln_g, gmlp_ln_b, gmlp_w_s, gmlp_b_s, conv_w, conv_b, conv_gn_g, conv_gn_b, w_out_ab, w_qkv_c, w_out_c, mem_wq, mem_wk, mem_wv, mem_wo, ffn_w1, ffn_w3, ffn_w2, ln_g, ln_b, loss_target, m_w_in_ab, m_gmlp_ln_g, m_gmlp_ln_b, m_gmlp_w_s, m_gmlp_b_s, m_conv_w, m_conv_b, m_conv_gn_g, m_conv_gn_b, m_w_out_ab, m_w_qkv_c, m_w_out_c, m_mem_wq, m_mem_wk, m_mem_wv, m_mem_wo, m_ffn_w1, m_ffn_w3, m_ffn_w2, m_ln_g, m_ln_b, v_w_in_ab, v_gmlp_ln_g, v_gmlp_ln_b, v_gmlp_w_s, v_gmlp_b_s, v_conv_w, v_conv_b, v_conv_gn_g, v_conv_gn_b, v_w_out_ab, v_w_qkv_c, v_w_out_c, v_mem_wq, v_mem_wk, v_mem_wv, v_mem_wo, v_ffn_w1, v_ffn_w3, v_ffn_w2, v_ln_g, v_ln_b):
    w = dict(w_in_ab=w_in_ab, gmlp_ln_g=gmlp_ln_g, gmlp_ln_b=gmlp_ln_b, gmlp_w_s=gmlp_w_s, gmlp_b_s=gmlp_b_s,
             conv_w=conv_w, conv_b=conv_b, conv_gn_g=conv_gn_g, conv_gn_b=conv_gn_b, w_out_ab=w_out_ab,
             w_qkv_c=w_qkv_c, w_out_c=w_out_c, mem_wq=mem_wq, mem_wk=mem_wk, mem_wv=mem_wv, mem_wo=mem_wo,
             ffn_w1=ffn_w1, ffn_w3=ffn_w3, ffn_w2=ffn_w2, ln_g=ln_g, ln_b=ln_b)
    m = dict(w_in_ab=m_w_in_ab, gmlp_ln_g=m_gmlp_ln_g, gmlp_ln_b=m_gmlp_ln_b, gmlp_w_s=m_gmlp_w_s,
             gmlp_b_s=m_gmlp_b_s, conv_w=m_conv_w, conv_b=m_conv_b, conv_gn_g=m_conv_gn_g,
             conv_gn_b=m_conv_gn_b, w_out_ab=m_w_out_ab, w_qkv_c=m_w_qkv_c, w_out_c=m_w_out_c, mem_wq=m_mem_wq,
             mem_wk=m_mem_wk, mem_wv=m_mem_wv, mem_wo=m_mem_wo, ffn_w1=m_ffn_w1, ffn_w3=m_ffn_w3,
             ffn_w2=m_ffn_w2, ln_g=m_ln_g, ln_b=m_ln_b)
    v = dict(w_in_ab=v_w_in_ab, gmlp_ln_g=v_gmlp_ln_g, gmlp_ln_b=v_gmlp_ln_b, gmlp_w_s=v_gmlp_w_s,
             gmlp_b_s=v_gmlp_b_s, conv_w=v_conv_w, conv_b=v_conv_b, conv_gn_g=v_conv_gn_g,
             conv_gn_b=v_conv_gn_b, w_out_ab=v_w_out_ab, w_qkv_c=v_w_qkv_c, w_out_c=v_w_out_c, mem_wq=v_mem_wq,
             mem_wk=v_mem_wk, mem_wv=v_mem_wv, mem_wo=v_mem_wo, ffn_w1=v_ffn_w1, ffn_w3=v_ffn_w3,
             ffn_w2=v_ffn_w2, ln_g=v_ln_g, ln_b=v_ln_b)

    w, m, v = ({n: _to_padded(n, t[n]) for n in _WEIGHTS} for t in (w, m, v))
    full = _gather_weights(w)
    loss_part, grad_x, grads = _local_step(x, mem, loss_target, full)
    loss = lax.psum(jnp.sum(loss_part), AXES)

    by_core = _pack_grads(grads)
    chip_sum = _pair_add(by_core, _pair_swap(by_core, "swap_grads_in_chip"))
    parts = _chip_exchange(chip_sum, "exchange_grads")
    packed = _adamw(parts, _pack_local(w), _pack_local(m), _pack_local(v))
    g_o, d_o, m_o, v_o = ({n: _from_padded(n, a) for n, a in _unpack_local(t, w).items()} for t in packed)
    return (loss, grad_x, *[g_o[n] for n in _WEIGHTS], *[d_o[n] for n in _WEIGHTS],
            *[m_o[n] for n in _WEIGHTS], *[v_o[n] for n in _WEIGHTS])
```

```python
import functools

import jax
import jax.numpy as jnp
from jax import lax
from jax.experimental import pallas as pl
from jax.experimental.pallas import tpu as pltpu

F32 = jnp.float32
BF16 = jnp.bfloat16
_MXU = jnp.bfloat16

D_MODEL = 1024
DEPTH = 4
MEM_LEN = 256
CHUNK = 128
A_GROUPS = 4
A_WIDTH = 512
B_WIDTH = 512
GROUP_DIM = 128
CONV_WIDTH = 31
C_HEADS = 16
C_HEAD_DIM = 64
MEM_HEADS = 4
MEM_HEAD_DIM = 256
D_FF = 2816
ALPHA = (2.0 * DEPTH) ** 0.25
LN_EPS = 1e-5
SB_SCALE = C_HEAD_DIM ** -0.5
MEM_SCALE = MEM_HEAD_DIM ** -0.5

ADAM_LR = 0.001
ADAM_B1 = 0.9
ADAM_B2 = 0.999
ADAM_EPS = 1e-08
ADAM_WD = 0.01
ADAM_STEP = 10

N_DEV = 8
AXES = ("x", "y", "c")
LANE = 128
SUBLANE = 8
PACK_COLS = 1024
_VMEM_LIMIT = 56 * 1024 * 1024

SB_TQ = 256
SB_TK = 128
SB_HEADS = 2
SB_BWD_HEADS = 2
D_FF_SHARD = D_FF // 8
D_FF_SHARD_PAD = 384
CONV_ROWS = 256
CONV_PAD = 32


def _tile(n, cap, mult):
    best = None
    for d in range(mult, min(n, cap) + 1, mult):
        if n % d == 0:
            best = d
    return n if best is None else best


def _call(body, *, name, grid, in_specs, out_specs, out_shape, scratch=(), sem=None):
    return pl.pallas_call(
        body, name=name, grid=grid, in_specs=in_specs, out_specs=out_specs, out_shape=out_shape,
        scratch_shapes=list(scratch),
        compiler_params=pltpu.CompilerParams(dimension_semantics=sem, vmem_limit_bytes=_VMEM_LIMIT))


def _dg(a, b, ca, cb):
    return lax.dot_general(a.astype(_MXU), b.astype(_MXU), (((ca,), (cb,)), ((), ())),
                           preferred_element_type=F32)


@jax.custom_vjp
def _dot(a, b):
    return _dg(a, b, 1, 0)


_dot.defvjp(lambda a, b: (_dg(a, b, 1, 0), (a, b)),
            lambda r, g: (_dg(g, r[1], 1, 1), _dg(r[0], g, 0, 0)))


@jax.custom_vjp
def _dot_t(a, b):
    return _dg(a, b, 1, 1)


_dot_t.defvjp(lambda a, b: (_dg(a, b, 1, 1), (a, b)),
              lambda r, g: (_dg(g, r[1], 1, 0), _dg(g, r[0], 0, 0)))


def _norm(z, g, b):
    mu = jnp.mean(z, axis=-1, keepdims=True)
    zc = z - mu
    var = jnp.mean(zc * zc, axis=-1, keepdims=True)
    return zc * lax.rsqrt(var + LN_EPS) * g + b


def _gelu(x):
    return 0.5 * x * (1.0 + lax.erf(x * (0.5 ** 0.5)))


def _matmul(a, b, mode="nn", add=None, out_dtype=F32, la=None, lb=None, name="mm"):
    ash, bsh = a.shape[-2:], b.shape[-2:]
    if mode == "nn":
        (M, K), (K2, N) = ash, bsh
    elif mode == "nt":
        (M, K), (N, K2) = ash, bsh
    else:
        (K, M), (K2, N) = ash, bsh
    assert K == K2, (a.shape, b.shape, mode)
    tm = _tile(M, 512, LANE)
    tn = _tile(N, 1408, LANE)
    tk = _tile(K, 512 if mode == "tn" else 1408, LANE)
    nk = K // tk
    if mode == "tn":
        a_blk, a_idx = (tk, tm), (lambda i, j, k: (k, i))
    else:
        a_blk, a_idx = (tm, tk), (lambda i, j, k: (i, k))
    if mode == "nt":
        b_blk, b_idx = (tn, tk), (lambda i, j, k: (j, k))
    else:
        b_blk, b_idx = (tk, tn), (lambda i, j, k: (k, j))
    dims = {"nn": (1, 0), "nt": (1, 1), "tn": (0, 0)}[mode]

    def spec(blk, idx, lead):
        if lead is None:
            return pl.BlockSpec(blk, idx)
        return pl.BlockSpec((None,) + blk, lambda i, j, k: (lead,) + idx(i, j, k))

    has_add = add is not None

    def body(*refs):
        a_ref, b_ref = refs[0], refs[1]
        add_ref = refs[2] if has_add else None
        o_ref = refs[3] if has_add else refs[2]
        p = _dg(a_ref[...], b_ref[...], *dims)

        def finish(r):
            if has_add:
                r = r + add_ref[...].astype(F32)
            o_ref[...] = r.astype(out_dtype)

        if nk == 1:
            finish(p)
        else:
            acc = refs[-1]
            k = pl.program_id(2)

            @pl.when(k == 0)
            def _():
                acc[...] = p

            @pl.when(k > 0)
            def _():
                acc[...] += p

            @pl.when(k == nk - 1)
            def _():
                finish(acc[...])

    in_specs = [spec(a_blk, a_idx, la), spec(b_blk, b_idx, lb)]
    args = [a, b]
    if has_add:
        in_specs.append(pl.BlockSpec((tm, tn), lambda i, j, k: (i, j)))
        args.append(add)
    return _call(
        body, name=name, grid=(M // tm, N // tn, nk), in_specs=in_specs,
        out_specs=pl.BlockSpec((tm, tn), lambda i, j, k: (i, j)),
        out_shape=jax.ShapeDtypeStruct((M, N), out_dtype),
        scratch=[pltpu.VMEM((tm, tn), F32)] if nk > 1 else [],
        sem=("parallel", "parallel", "arbitrary"))(*args)


def _ln_fn(x, f, g, b):
    return _norm(ALPHA * x + f, g, b)


def _ln_fwd(x, f, g, b):
    T, D = x.shape
    tm = _tile(T, 256, SUBLANE)

    def body(x_ref, f_ref, g_ref, b_ref, o_ref):
        o_ref[...] = _ln_fn(x_ref[...], f_ref[...], g_ref[...], b_ref[...])

    row = pl.BlockSpec((tm, D), lambda i: (i, 0))
    par = pl.BlockSpec((1, D), lambda i: (0, 0))
    return _call(body, name="ln_fwd", grid=(T // tm,), in_specs=[row, row, par, par], out_specs=row,
                 out_shape=jax.ShapeDtypeStruct((T, D), F32), sem=("parallel",))(x, f, g, b)


def _ln_bwd(x, f, g, b, dy):
    T, D = x.shape
    tm = _tile(T, 256, SUBLANE)

    def body(x_ref, f_ref, g_ref, b_ref, dy_ref, dx_ref, df_ref, dg_ref, db_ref):
        _, vjp = jax.vjp(_ln_fn, x_ref[...], f_ref[...], g_ref[...], b_ref[...])
        dx, df, dg, db = vjp(dy_ref[...])
        dx_ref[...] = dx
        df_ref[...] = df.astype(BF16)

        @pl.when(pl.program_id(0) == 0)
        def _():
            dg_ref[...] = jnp.zeros_like(dg_ref)
            db_ref[...] = jnp.zeros_like(db_ref)

        dg_ref[...] += dg
        db_ref[...] += db

    row = pl.BlockSpec((tm, D), lambda i: (i, 0))
    par = pl.BlockSpec((1, D), lambda i: (0, 0))
    return _call(body, name="ln_bwd", grid=(T // tm,), in_specs=[row, row, par, par, row],
                 out_specs=[row, row, par, par],
                 out_shape=[jax.ShapeDtypeStruct((T, D), F32), jax.ShapeDtypeStruct((T, D), BF16),
                            jax.ShapeDtypeStruct((1, D), F32), jax.ShapeDtypeStruct((1, D), F32)],
                 sem=("arbitrary",))(x, f, g, b, dy)


def _loss_fwd(y, tgt):
    T, D = y.shape
    tm = _tile(T, 256, SUBLANE)

    def body(y_ref, t_ref, l_ref, dy_ref):
        d = y_ref[...] - t_ref[...]
        dy_ref[...] = d * (1.0 / D)

        @pl.when(pl.program_id(0) == 0)
        def _():
            l_ref[...] = jnp.zeros_like(l_ref)

        l_ref[...] += jnp.sum(d * d, axis=0, keepdims=True) * (0.5 / D)

    row = pl.BlockSpec((tm, D), lambda i: (i, 0))
    par = pl.BlockSpec((1, D), lambda i: (0, 0))
    return _call(body, name="loss", grid=(T // tm,), in_specs=[row, row], out_specs=[par, row],
                 out_shape=[jax.ShapeDtypeStruct((1, D), F32), jax.ShapeDtypeStruct((T, D), F32)],
                 sem=("arbitrary",))(y, tgt)


def _swiglu_fn(h1, h3):
    return h1 * jax.nn.sigmoid(h1) * h3


def _swiglu_fwd(h13):
    T, F2 = h13.shape
    F = F2 // 2
    tm = _tile(T, 256, SUBLANE)

    def body(h1_ref, h3_ref, o_ref):
        o_ref[...] = _swiglu_fn(h1_ref[...], h3_ref[...]).astype(BF16)

    return _call(body, name="swiglu_fwd", grid=(T // tm,),
                 in_specs=[pl.BlockSpec((tm, F), lambda i: (i, 0)), pl.BlockSpec((tm, F), lambda i: (i, 1))],
                 out_specs=pl.BlockSpec((tm, F), lambda i: (i, 0)),
                 out_shape=jax.ShapeDtypeStruct((T, F), BF16), sem=("parallel",))(h13, h13)


def _swiglu_bwd(h13, dact):
    T, F2 = h13.shape
    F = F2 // 2
    tm = _tile(T, 256, SUBLANE)

    def body(h1_ref, h3_ref, d_ref, o_ref):
        _, vjp = jax.vjp(_swiglu_fn, h1_ref[...], h3_ref[...])
        d1, d3 = vjp(d_ref[...])
        o_ref[:, :F] = d1.astype(BF16)
        o_ref[:, F:] = d3.astype(BF16)

    return _call(body, name="swiglu_bwd", grid=(T // tm,),
                 in_specs=[pl.BlockSpec((tm, F), lambda i: (i, 0)), pl.BlockSpec((tm, F), lambda i: (i, 1)),
                           pl.BlockSpec((tm, F), lambda i: (i, 0))],
                 out_specs=pl.BlockSpec((tm, F2), lambda i: (i, 0)),
                 out_shape=jax.ShapeDtypeStruct((T, F2), BF16), sem=("parallel",))(h13, h13, dact)


@jax.custom_vjp
def _chunkmix(wm, vn, bs):
    n = vn.shape[0] // CHUNK
    return jnp.concatenate([_dg(wm, vn[c * CHUNK:(c + 1) * CHUNK], 1, 0) + bs for c in range(n)], axis=0)


def _chunkmix_fwd(wm, vn, bs):
    return _chunkmix(wm, vn, bs), (wm, vn)


def _chunkmix_bwd(res, ct):
    wm, vn = res
    n = vn.shape[0] // CHUNK
    cts = [ct[c * CHUNK:(c + 1) * CHUNK] for c in range(n)]
    dvn = jnp.concatenate([_dg(wm, cts[c], 0, 0) for c in range(n)], axis=0)
    dwm = sum(_dg(cts[c], vn[c * CHUNK:(c + 1) * CHUNK], 1, 1) for c in range(n))
    dbs = sum(jnp.sum(cts[c], axis=1, keepdims=True) for c in range(n))
    return dwm, dvn, dbs


_chunkmix.defvjp(_chunkmix_fwd, _chunkmix_bwd)


def _gmlp_fn(u, v, lg, lb, ws, bs):
    ug = _gelu(u)
    vn = _norm(_gelu(v), lg, lb)
    r = lax.broadcasted_iota(jnp.int32, (CHUNK, CHUNK), 0)
    c = lax.broadcasted_iota(jnp.int32, (CHUNK, CHUNK), 1)
    wm = jnp.where(r >= c, ws, 0.0)
    return ug * _chunkmix(wm, vn, bs)


def _gmlp_specs(tm, order):
    def at(col0):
        return pl.BlockSpec((tm, GROUP_DIM), lambda *ids: (order(*ids)[0], col0 + order(*ids)[1]))
    par = pl.BlockSpec((1, GROUP_DIM), lambda *ids: (0, order(*ids)[1]))
    ws = pl.BlockSpec((None, CHUNK, CHUNK), lambda *ids: (order(*ids)[1], 0, 0))
    bs = pl.BlockSpec((None, CHUNK, 1), lambda *ids: (order(*ids)[1], 0, 0))
    return at, par, ws, bs


def _gmlp_fwd(hin, lg, lb, ws, bs):
    T = hin.shape[0]
    tm = _tile(T, 512, CHUNK)
    at, par, wsp, bsp = _gmlp_specs(tm, lambda i, g: (i, g))

    def body(u_ref, v_ref, lg_ref, lb_ref, ws_ref, bs_ref, o_ref):
        o_ref[...] = _gmlp_fn(u_ref[...], v_ref[...], lg_ref[...], lb_ref[...], ws_ref[...],
                              bs_ref[...]).astype(BF16)

    return _call(body, name="gmlp_fwd", grid=(T // tm, A_GROUPS),
                 in_specs=[at(0), at(A_GROUPS), par, par, wsp, bsp], out_specs=at(0),
                 out_shape=jax.ShapeDtypeStruct((T, A_WIDTH), BF16),
                 sem=("parallel", "parallel"))(hin, hin, lg, lb, ws, bs)


def _gmlp_bwd(hin, lg, lb, ws, bs, dyab):
    T = hin.shape[0]
    tm = _tile(T, 512, CHUNK)
    at, par, wsp, bsp = _gmlp_specs(tm, lambda g, i: (i, g))

    def body(u_ref, v_ref, lg_ref, lb_ref, ws_ref, bs_ref, dy_ref,
             du_ref, dv_ref, dlg_ref, dlb_ref, dws_ref, dbs_ref):
        _, vjp = jax.vjp(_gmlp_fn, u_ref[...], v_ref[...], lg_ref[...], lb_ref[...], ws_ref[...], bs_ref[...])
        du, dv, dlg, dlb, dws, dbs = vjp(dy_ref[...])
        du_ref[...] = du.astype(BF16)
        dv_ref[...] = dv.astype(BF16)

        @pl.when(pl.program_id(1) == 0)
        def _():
            dlg_ref[...] = jnp.zeros_like(dlg_ref)
            dlb_ref[...] = jnp.zeros_like(dlb_ref)
            dws_ref[...] = jnp.zeros_like(dws_ref)
            dbs_ref[...] = jnp.zeros_like(dbs_ref)

        dlg_ref[...] += dlg
        dlb_ref[...] += dlb
        dws_ref[...] += dws
        dbs_ref[...] += dbs

    half = jax.ShapeDtypeStruct((T, A_WIDTH), BF16)
    return _call(body, name="gmlp_bwd", grid=(A_GROUPS, T // tm),
                 in_specs=[at(0), at(A_GROUPS), par, par, wsp, bsp, at(0)],
                 out_specs=[at(0), at(0), par, par, wsp, bsp],
                 out_shape=[half, half, jax.ShapeDtypeStruct((1, A_WIDTH), F32),
                            jax.ShapeDtypeStruct((1, A_WIDTH), F32),
                            jax.ShapeDtypeStruct((A_GROUPS, CHUNK, CHUNK), F32),
                            jax.ShapeDtypeStruct((A_GROUPS, CHUNK, 1), F32)],
                 sem=("parallel", "arbitrary"))(hin, hin, lg, lb, ws, bs, dyab)


def _glu_fn(a, gt):
    return a * jax.nn.sigmoid(gt)


def _gn_silu_fn(c, cb, gg, gb):
    y = _norm(c + cb, gg, gb)
    return y * jax.nn.sigmoid(y)


def _conv_taps(w_ref, src_ref, row0, first):
    acc = None
    for k in range(CONV_WIDTH):
        term = w_ref[pl.ds(k, 1), :] * src_ref[pl.ds(row0 + first(k), CONV_ROWS), :]
        acc = term if acc is None else acc + term
    return acc


def _conv_specs(S, order):
    def at(col0):
        return pl.BlockSpec((S, GROUP_DIM), lambda *ids: (order(*ids)[0], col0 + order(*ids)[1]))
    par = pl.BlockSpec((1, GROUP_DIM), lambda *ids: (0, order(*ids)[1]))
    cw = pl.BlockSpec((CONV_WIDTH, GROUP_DIM), lambda *ids: (0, order(*ids)[1]))
    return at, par, cw


def _conv_fwd(hin, cw, cb, gg, gb, n_ex, S):
    T = hin.shape[0]
    at, par, cwp = _conv_specs(S, lambda e, g: (e, g))
    ngrp = B_WIDTH // GROUP_DIM
    lead = CONV_PAD - (CONV_WIDTH - 1)

    def body(a_ref, gt_ref, cw_ref, cb_ref, gg_ref, gb_ref, o_ref, pad_ref):
        pad_ref[0:CONV_PAD, :] = jnp.zeros((CONV_PAD, GROUP_DIM), F32)
        for r in range(S // CONV_ROWS):
            rows = pl.ds(r * CONV_ROWS, CONV_ROWS)
            pad_ref[pl.ds(CONV_PAD + r * CONV_ROWS, CONV_ROWS), :] = _glu_fn(a_ref[rows, :], gt_ref[rows, :])
        for r in range(S // CONV_ROWS):
            c = _conv_taps(cw_ref, pad_ref, r * CONV_ROWS, lambda k: lead + k)
            o_ref[pl.ds(r * CONV_ROWS, CONV_ROWS), :] = _gn_silu_fn(
                c, cb_ref[...], gg_ref[...], gb_ref[...]).astype(BF16)

    return _call(body, name="conv_fwd", grid=(n_ex, ngrp),
                 in_specs=[at(2 * A_GROUPS), at(2 * A_GROUPS + ngrp), cwp, par, par, par], out_specs=at(0),
                 out_shape=jax.ShapeDtypeStruct((T, B_WIDTH), BF16),
                 scratch=[pltpu.VMEM((S + CONV_PAD, GROUP_DIM), F32)],
                 sem=("parallel", "parallel"))(hin, hin, cw, cb, gg, gb)


def _conv_bwd(hin, cw, cb, gg, gb, dyab, n_ex, S):
    T = hin.shape[0]
    at, par, cwp = _conv_specs(S, lambda g, e: (e, g))
    ngrp = B_WIDTH // GROUP_DIM
    lead = CONV_PAD - (CONV_WIDTH - 1)
    nchunk = S // CONV_ROWS

    def body(a_ref, gt_ref, cw_ref, cb_ref, gg_ref, gb_ref, dy_ref,
             da_ref, dgt_ref, dcw_ref, dcb_ref, dgg_ref, dgb_ref, pad_ref, dcp_ref, wacc_ref, pacc_ref):
        @pl.when(pl.program_id(1) == 0)
        def _():
            wacc_ref[...] = jnp.zeros_like(wacc_ref)
            pacc_ref[...] = jnp.zeros_like(pacc_ref)

        pad_ref[0:CONV_PAD, :] = jnp.zeros((CONV_PAD, GROUP_DIM), F32)
        dcp_ref[S:S + CONV_PAD, :] = jnp.zeros((CONV_PAD, GROUP_DIM), F32)
        for r in range(nchunk):
            rows = pl.ds(r * CONV_ROWS, CONV_ROWS)
            pad_ref[pl.ds(CONV_PAD + r * CONV_ROWS, CONV_ROWS), :] = _glu_fn(a_ref[rows, :], gt_ref[rows, :])
        for r in range(nchunk):
            rows = pl.ds(r * CONV_ROWS, CONV_ROWS)
            c = _conv_taps(cw_ref, pad_ref, r * CONV_ROWS, lambda k: lead + k)
            _, vjp = jax.vjp(_gn_silu_fn, c, cb_ref[...], gg_ref[...], gb_ref[...])
            dc, dcb, dgg, dgb = vjp(dy_ref[rows, :])
            dcp_ref[rows, :] = dc
            pacc_ref[0:1, :] += dcb
            pacc_ref[1:2, :] += dgg
            pacc_ref[2:3, :] += dgb
        for r in range(nchunk):
            rows = pl.ds(r * CONV_ROWS, CONV_ROWS)
            dh = _conv_taps(cw_ref, dcp_ref, r * CONV_ROWS, lambda k: CONV_WIDTH - 1 - k)
            _, vjp = jax.vjp(_glu_fn, a_ref[rows, :], gt_ref[rows, :])
            da, dgt = vjp(dh)
            da_ref[rows, :] = da.astype(BF16)
            dgt_ref[rows, :] = dgt.astype(BF16)
            dc = dcp_ref[rows, :]
            for k in range(CONV_WIDTH):
                prod = dc * pad_ref[pl.ds(r * CONV_ROWS + lead + k, CONV_ROWS), :]
                wacc_ref[k] += jnp.sum(prod.reshape(CONV_ROWS // SUBLANE, SUBLANE, GROUP_DIM), axis=0)
        for k in range(CONV_WIDTH):
            dcw_ref[pl.ds(k, 1), :] = jnp.sum(wacc_ref[k], axis=0, keepdims=True)
        dcb_ref[...] = pacc_ref[0:1, :]
        dgg_ref[...] = pacc_ref[1:2, :]
        dgb_ref[...] = pacc_ref[2:3, :]

    half = jax.ShapeDtypeStruct((T, B_WIDTH), BF16)
    vec = jax.ShapeDtypeStruct((1, B_WIDTH), F32)
    return _call(body, name="conv_bwd", grid=(ngrp, n_ex),
                 in_specs=[at(2 * A_GROUPS), at(2 * A_GROUPS + ngrp), cwp, par, par, par, at(ngrp)],
                 out_specs=[at(0), at(0), cwp, par, par, par],
                 out_shape=[half, half, jax.ShapeDtypeStruct((CONV_WIDTH, B_WIDTH), F32), vec, vec, vec],
                 scratch=[pltpu.VMEM((S + CONV_PAD, GROUP_DIM), F32), pltpu.VMEM((S + CONV_PAD, GROUP_DIM), F32),
                          pltpu.VMEM((CONV_WIDTH, SUBLANE, GROUP_DIM), F32), pltpu.VMEM((SUBLANE, GROUP_DIM), F32)],
                 sem=("parallel", "arbitrary"))(hin, hin, cw, cb, gg, gb, dyab)


def _sb_consts():
    r = lax.broadcasted_iota(jnp.int32, (SB_TK + SUBLANE, 2 * SB_TK), 0)
    c = lax.broadcasted_iota(jnp.int32, (SB_TK + SUBLANE, 2 * SB_TK), 1) % SB_TK
    tail = r >= SB_TK
    u_after = jnp.where((c > r) | tail, 1.0, 0.0).astype(BF16)
    u_before = jnp.where((c < r) | tail, 1.0, 0.0).astype(BF16)
    s = lax.broadcasted_iota(jnp.int32, (SB_TK, SB_TQ), 0)
    t = lax.broadcasted_iota(jnp.int32, (SB_TK, SB_TQ), 1)
    masks = [(s + SB_TK * d) < t for d in range(SB_TQ // SB_TK)]
    return u_after, u_before, masks


def _sb_store_split(hl, h, j, x):
    hi = x.astype(BF16)
    hl[h, j, :SB_TK] = hi
    hl[h, j, SB_TK:] = (x - hi.astype(F32)).astype(BF16)


def _split_sum(u, x):
    hi = x.astype(BF16)
    lo = (x - hi.astype(F32)).astype(BF16)
    res = _dg(u, jnp.concatenate([hi, lo], axis=0), 1, 0)
    return res[:SB_TK], res[SB_TK:]


def _sb_sums(u, hl, ws, tot, h, j):
    res = _dg(u, hl[h, j], 1, 0)
    ws[h, j] = res[:SB_TK]
    tot[h, j] = res[SB_TK:]


def _sb_tiles(trips, per, fn, carry=None, descending=False):
    def trip(t, c):
        t = trips - 1 - t if descending else t
        for u in (reversed(range(per)) if descending else range(per)):
            if carry is None:
                fn(per * t + u)
            else:
                c = fn(per * t + u, c)
        return c

    return lax.fori_loop(0, trips, trip, 0 if carry is None else carry)


def _add8(x, row8):
    return (x.reshape(-1, SUBLANE, x.shape[-1]) + row8[None]).reshape(x.shape)


def _sb_blocked(dst, src, h, n, width):
    for t in range(n):
        dst[h, t] = src[h, :, t * width:(t + 1) * width].astype(dst.dtype)


def _sb_spec(S, nh):
    def at(part):
        return pl.BlockSpec((nh, C_HEAD_DIM, S), lambda e, h: (part * (C_HEADS // nh) + h, 0, e))
    return at


def _sb_fwd(qkvT, n_ex, S):
    T = qkvT.shape[-1]
    nq, nk = S // SB_TQ, S // SB_TK
    per = SB_TQ // SB_TK
    heads = range(SB_HEADS)

    def body(q_ref, k_ref, v_ref, o_ref, qb, kb, vb, ob, zs, hl, ws, tot, ab):
        for h in heads:
            _sb_blocked(qb, q_ref, h, nq, SB_TQ)
            _sb_blocked(kb, k_ref, h, nk, SB_TK)
            _sb_blocked(vb, v_ref, h, nk, SB_TK)
        u_after, _, masks = _sb_consts()

        def soft(h, j, mask):
            z = zs[h, j]
            sp = jnp.maximum(z, 0.0) + jnp.log1p(jnp.exp(-jnp.abs(z)))
            _sb_store_split(hl, h, j, -sp if mask is None else jnp.where(mask, -sp, 0.0))
            zs[h, j] = z - sp

        def weigh(h, j, carry, mask):
            att = jnp.exp(zs[h, j] + _add8(ws[h, j], carry))
            if mask is not None:
                att = jnp.where(mask, att, 0.0)
            ab[h, j] = att.astype(_MXU)
            return carry + tot[h, j]

        def qtile(i, _):
            qs = [qb[h, i] for h in heads]

            def scores(j):
                for h in heads:
                    zs[h, j] = _dg(kb[h, j], qs[h], 0, 0) * SB_SCALE

            _sb_tiles(i + 1, per, scores)
            _sb_tiles(i, per, lambda j: [soft(h, j, None) for h in heads])
            for d in range(per):
                for h in heads:
                    soft(h, per * i + d, masks[d])
            _sb_tiles(i + 1, per, lambda j: [_sb_sums(u_after, hl, ws, tot, h, j) for h in heads])
            carry = tuple(jnp.zeros((SUBLANE, SB_TQ), F32) for h in heads)
            for d in reversed(range(per)):
                carry = tuple(weigh(h, per * i + d, carry[h], masks[d]) for h in heads)
            _sb_tiles(i, per, lambda j, c: tuple(weigh(h, j, c[h], None) for h in heads), carry, descending=True)
            acc = tuple(jnp.zeros((C_HEAD_DIM, SB_TQ), F32) for h in heads)
            acc = _sb_tiles(i + 1, per, lambda j, a: tuple(a[h] + _dg(vb[h, j], ab[h, j], 1, 0) for h in heads), acc)
            for h in heads:
                ob[h, i] = acc[h]
            return 0

        lax.fori_loop(0, nq, qtile, 0)
        for h in heads:
            for i in range(nq):
                o_ref[h, :, i * SB_TQ:(i + 1) * SB_TQ] = ob[h, i].astype(BF16)

    at = _sb_spec(S, SB_HEADS)
    qshape, kshape = (SB_HEADS, nq, C_HEAD_DIM, SB_TQ), (SB_HEADS, nk, C_HEAD_DIM, SB_TK)
    tiles = (SB_HEADS, nk, SB_TK, SB_TQ)
    return _call(body, name="sb_fwd", grid=(n_ex, C_HEADS // SB_HEADS), in_specs=[at(0), at(1), at(2)],
                 out_specs=at(0), out_shape=jax.ShapeDtypeStruct((C_HEADS, C_HEAD_DIM, T), BF16),
                 scratch=[pltpu.VMEM(qshape, _MXU), pltpu.VMEM(kshape, _MXU), pltpu.VMEM(kshape, _MXU),
                          pltpu.VMEM(qshape, F32), pltpu.VMEM(tiles, F32),
                          pltpu.VMEM((SB_HEADS, nk, 2 * SB_TK, SB_TQ), BF16), pltpu.VMEM(tiles, F32),
                          pltpu.VMEM((SB_HEADS, nk, SUBLANE, SB_TQ), F32), pltpu.VMEM(tiles, _MXU)],
                 sem=("parallel", "parallel"))(qkvT, qkvT, qkvT)


def _sb_bwd(qkvT, doT, n_ex, S):
    T = qkvT.shape[-1]
    nq, nk = S // SB_TQ, S // SB_TK
    per = SB_TQ // SB_TK
    heads = range(SB_BWD_HEADS)

    def body(q_ref, k_ref, v_ref, do_ref, dq_ref, dk_ref, dv_ref, qb, kb, vb, dob, dqa, dka, dva, dl_s, sg_s):
        for h in heads:
            _sb_blocked(qb, q_ref, h, nq, SB_TQ)
            _sb_blocked(dob, do_ref, h, nq, SB_TQ)
            _sb_blocked(kb, k_ref, h, nk, SB_TK)
            _sb_blocked(vb, v_ref, h, nk, SB_TK)
        dka[...] = jnp.zeros_like(dka)
        dva[...] = jnp.zeros_like(dva)
        u_after, u_before, masks = _sb_consts()

        def rebuild(h, j, qi, gi, carry, mask):
            z = _dg(kb[h, j], qi, 0, 0) * SB_SCALE
            e = jnp.exp(-jnp.abs(z))
            sp = jnp.maximum(z, 0.0) + jnp.log1p(e)
            within, total = _split_sum(u_after, -sp if mask is None else jnp.where(mask, -sp, 0.0))
            att = jnp.exp((z - sp) + _add8(within, carry))
            r = 1.0 / (1.0 + e)
            sig = jnp.where(z >= 0.0, r, e * r)
            if mask is not None:
                att = jnp.where(mask, att, 0.0)
                sig = jnp.where(mask, sig, 0.0)
            dl_s[h, j] = _dg(vb[h, j], gi, 0, 0) * att
            sg_s[h, j] = sig
            dva[h, j] += _dg(gi, att, 1, 1)
            return carry + total

        def push(h, j, qi, carry, dq):
            dlogit, sig = dl_s[h, j], sg_s[h, j]
            within, total = _split_sum(u_before, dlogit)
            dz = (dlogit * (1.0 - sig) - sig * _add8(within, carry)) * SB_SCALE
            dka[h, j] += _dg(qi, dz, 1, 1)
            return carry + total, dq + _dg(kb[h, j], dz, 1, 0)

        def qtile(i, _):
            qs = [qb[h, i] for h in heads]
            gs = [dob[h, i] for h in heads]
            carry = tuple(jnp.zeros((SUBLANE, SB_TQ), F32) for h in heads)
            for d in reversed(range(per)):
                carry = tuple(rebuild(h, per * i + d, qs[h], gs[h], carry[h], masks[d]) for h in heads)
            _sb_tiles(i, per, lambda j, c: tuple(rebuild(h, j, qs[h], gs[h], c[h], None) for h in heads), carry,
                      descending=True)
            st = tuple((jnp.zeros((SUBLANE, SB_TQ), F32), jnp.zeros((C_HEAD_DIM, SB_TQ), F32)) for h in heads)
            st = _sb_tiles(i + 1, per, lambda j, s: tuple(push(h, j, qs[h], *s[h]) for h in heads), st)
            for h in heads:
                dqa[h, i] = st[h][1]
            return 0

        lax.fori_loop(0, nq, qtile, 0)
        for h in heads:
            for i in range(nq):
                dq_ref[h, :, i * SB_TQ:(i + 1) * SB_TQ] = dqa[h, i].astype(BF16)
            for j in range(nk):
                dk_ref[h, :, j * SB_TK:(j + 1) * SB_TK] = dka[h, j].astype(BF16)
                dv_ref[h, :, j * SB_TK:(j + 1) * SB_TK] = dva[h, j].astype(BF16)

    nh = SB_BWD_HEADS
    at = _sb_spec(S, nh)
    out = jax.ShapeDtypeStruct((C_HEADS, C_HEAD_DIM, T), BF16)
    qshape, kshape = (nh, nq, C_HEAD_DIM, SB_TQ), (nh, nk, C_HEAD_DIM, SB_TK)
    tiles = (nh, nk, SB_TK, SB_TQ)
    return _call(body, name="sb_bwd", grid=(n_ex, C_HEADS // nh), in_specs=[at(0), at(1), at(2), at(0)],
                 out_specs=[at(0), at(0), at(0)], out_shape=[out, out, out],
                 scratch=[pltpu.VMEM(qshape, _MXU), pltpu.VMEM(kshape, _MXU), pltpu.VMEM(kshape, _MXU),
                          pltpu.VMEM(qshape, _MXU), pltpu.VMEM(qshape, F32), pltpu.VMEM(kshape, F32),
                          pltpu.VMEM(kshape, F32), pltpu.VMEM(tiles, F32), pltpu.VMEM(tiles, F32)],
                 sem=("parallel", "parallel"))(qkvT, qkvT, qkvT, doT)


def _xattn_fn(q, k, v):
    s = _dot_t(q, k) * MEM_SCALE
    m = lax.stop_gradient(jnp.max(s, axis=-1, keepdims=True))
    p = jnp.exp(s - m)
    p = p / jnp.sum(p, axis=-1, keepdims=True)
    return _dot(p, v)


def _xattn_fwd(q, kk, vv, n_ex, S):
    T = q.shape[0]
    tq = _tile(S, 1024, SUBLANE)
    nqt = S // tq

    def body(q_ref, k_ref, v_ref, o_ref):
        o_ref[...] = _xattn_fn(q_ref[...], k_ref[...], v_ref[...]).astype(BF16)

    qs = pl.BlockSpec((tq, MEM_HEAD_DIM), lambda e, i, h: (e * nqt + i, h))
    ks = pl.BlockSpec((MEM_LEN, MEM_HEAD_DIM), lambda e, i, h: (e, h))
    return _call(body, name="xattn_fwd", grid=(n_ex, nqt, MEM_HEADS), in_specs=[qs, ks, ks], out_specs=qs,
                 out_shape=jax.ShapeDtypeStruct((T, D_MODEL), BF16),
                 sem=("parallel", "parallel", "parallel"))(q, kk, vv)


def _xattn_bwd(q, kk, vv, do, n_ex, S):
    T = q.shape[0]
    tq = _tile(S, 1024, SUBLANE)
    nqt = S // tq

    def body(q_ref, k_ref, v_ref, do_ref, dq_ref, dk_ref, dv_ref):
        _, vjp = jax.vjp(_xattn_fn, q_ref[...].astype(F32), k_ref[...].astype(F32), v_ref[...].astype(F32))
        dq, dk, dv = vjp(do_ref[...].astype(F32))
        dq_ref[...] = dq.astype(BF16)

        @pl.when(pl.program_id(2) == 0)
        def _():
            dk_ref[...] = jnp.zeros_like(dk_ref)
            dv_ref[...] = jnp.zeros_like(dv_ref)

        dk_ref[...] += dk
        dv_ref[...] += dv

    qs = pl.BlockSpec((tq, MEM_HEAD_DIM), lambda e, h, i: (e * nqt + i, h))
    ks = pl.BlockSpec((MEM_LEN, MEM_HEAD_DIM), lambda e, h, i: (e, h))
    kv = jax.ShapeDtypeStruct(kk.shape, F32)
    return _call(body, name="xattn_bwd", grid=(n_ex, MEM_HEADS, nqt), in_specs=[qs, ks, ks, qs],
                 out_specs=[qs, ks, ks], out_shape=[jax.ShapeDtypeStruct((T, D_MODEL), BF16), kv, kv],
                 sem=("parallel", "parallel", "arbitrary"))(q, kk, vv, do)


def _local_step(x, mem, tgt, W):
    n_ex, S, D = x.shape
    T = n_ex * S
    h = x.reshape(T, D)
    mem2 = mem.reshape(n_ex * MEM_LEN, D)
    row = lambda v: v.reshape(1, -1)
    saved = []
    for l in range(DEPTH):
        sv = dict(x0=h)
        if l % 2 == 0:
            e = l // 2
            hin = _matmul(h, W["w_in_ab"], lb=e, name="in_proj")
            ya = _gmlp_fwd(hin, row(W["gmlp_ln_g"][e]), row(W["gmlp_ln_b"][e]), W["gmlp_w_s"][e],
                           W["gmlp_b_s"][e][:, :, None])
            yb = _conv_fwd(hin, W["conv_w"][e], row(W["conv_b"][e]), row(W["conv_gn_g"][e]),
                           row(W["conv_gn_b"][e]), n_ex, S)
            yab = jnp.concatenate([ya, yb], axis=1)
            mix = _matmul(yab, W["w_out_ab"], lb=e, name="out_proj")
            sv.update(hin=hin, yab=yab)
        else:
            o = l // 2
            qkvT = _matmul(W["w_qkv_cT"], h, mode="nt", la=o, out_dtype=BF16, name="qkv_proj")
            yT = _sb_fwd(qkvT.reshape(3 * C_HEADS, C_HEAD_DIM, T), n_ex, S)
            mix = _matmul(yT.reshape(D, T), W["w_out_c"], mode="tn", lb=o, name="sb_out_proj")
            sv.update(qkvT=qkvT, yT=yT)
        x1 = _ln_fwd(h, mix, row(W["ln_g"][l, 0]), row(W["ln_b"][l, 0]))
        q = _matmul(x1, W["mem_wq"], lb=l, out_dtype=BF16, name="mem_q")
        kk = _matmul(mem2, W["mem_wk"], lb=l, out_dtype=BF16, name="mem_kv")
        vv = _matmul(mem2, W["mem_wv"], lb=l, out_dtype=BF16, name="mem_kv")
        oc = _xattn_fwd(q, kk, vv, n_ex, S)
        cross = _matmul(oc, W["mem_wo"], lb=l, name="mem_o")
        x2 = _ln_fwd(x1, cross, row(W["ln_g"][l, 1]), row(W["ln_b"][l, 1]))
        h13 = _matmul(x2, W["ffn_w13"], lb=l, name="ffn_in")
        act = _swiglu_fwd(h13)
        f = _matmul(act, W["ffn_w2"], lb=l, name="ffn_out")
        x3 = _ln_fwd(x2, f, row(W["ln_g"][l, 2]), row(W["ln_b"][l, 2]))
        sv.update(mix=mix, x1=x1, q=q, kk=kk, vv=vv, oc=oc, cross=cross, x2=x2, h13=h13, act=act, f=f)
        saved.append(sv)
        h = x3

    loss_part, dh = _loss_fwd(h, tgt.reshape(T, D))

    G = {k: [None] * n for k, n in [
        ("w_in_ab", 2), ("gmlp_ln_g", 2), ("gmlp_ln_b", 2), ("gmlp_w_s", 2), ("gmlp_b_s", 2), ("conv_w", 2),
        ("conv_b", 2), ("conv_gn_g", 2), ("conv_gn_b", 2), ("w_out_ab", 2), ("w_qkv_c", 2), ("w_out_c", 2),
        ("mem_wq", 4), ("mem_wk", 4), ("mem_wv", 4), ("mem_wo", 4), ("ffn_w1", 4), ("ffn_w3", 4),
        ("ffn_w2", 4), ("ln_g", 4), ("ln_b", 4)]}
    for l in reversed(range(DEPTH)):
        sv = saved[l]
        lng, lnb = [None] * 3, [None] * 3
        dx2, df, lng[2], lnb[2] = _ln_bwd(sv["x2"], sv["f"], row(W["ln_g"][l, 2]), row(W["ln_b"][l, 2]), dh)
        dact = _matmul(df, W["ffn_w2"], mode="nt", lb=l, name="ffn_out_dx")
        G["ffn_w2"][l] = _matmul(sv["act"], df, mode="tn", name="ffn_out_dw")
        dh13 = _swiglu_bwd(sv["h13"], dact)
        dx2 = _matmul(dh13, W["ffn_w13"], mode="nt", lb=l, add=dx2, name="ffn_in_dx")
        dw13 = _matmul(sv["x2"], dh13, mode="tn", name="ffn_in_dw")
        ff = dw13.shape[1] // 2
        G["ffn_w1"][l], G["ffn_w3"][l] = dw13[:, :ff], dw13[:, ff:]
        dx1, dcross, lng[1], lnb[1] = _ln_bwd(sv["x1"], sv["cross"], row(W["ln_g"][l, 1]),
                                              row(W["ln_b"][l, 1]), dx2)
        doc = _matmul(dcross, W["mem_wo"], mode="nt", lb=l, out_dtype=BF16, name="mem_o_dx")
        G["mem_wo"][l] = _matmul(sv["oc"], dcross, mode="tn", name="mem_o_dw")
        dq, dkk, dvv = _xattn_bwd(sv["q"], sv["kk"], sv["vv"], doc, n_ex, S)
        dx1 = _matmul(dq, W["mem_wq"], mode="nt", lb=l, add=dx1, name="mem_q_dx")
        G["mem_wq"][l] = _matmul(sv["x1"], dq, mode="tn", name="mem_q_dw")
        G["mem_wk"][l] = _matmul(mem2, dkk, mode="tn", name="mem_kv_dw")
        G["mem_wv"][l] = _matmul(mem2, dvv, mode="tn", name="mem_kv_dw")
        dx0, dmix, lng[0], lnb[0] = _ln_bwd(sv["x0"], sv["mix"], row(W["ln_g"][l, 0]), row(W["ln_b"][l, 0]), dx1)
        if l % 2 == 0:
            e = l // 2
            dyab = _matmul(dmix, W["w_out_ab"], mode="nt", lb=e, name="out_proj_dx")
            G["w_out_ab"][e] = _matmul(sv["yab"], dmix, mode="tn", name="out_proj_dw")
            du, dv, dlg, dlb, dws, dbs = _gmlp_bwd(
                sv["hin"], row(W["gmlp_ln_g"][e]), row(W["gmlp_ln_b"][e]), W["gmlp_w_s"][e],
                W["gmlp_b_s"][e][:, :, None], dyab)
            da, dgt, dcw, dcb, dgg, dgb = _conv_bwd(
                sv["hin"], W["conv_w"][e], row(W["conv_b"][e]), row(W["conv_gn_g"][e]),
                row(W["conv_gn_b"][e]), dyab, n_ex, S)
            dhin = jnp.concatenate([du, dv, da, dgt], axis=1)
            dh = _matmul(dhin, W["w_in_ab"], mode="nt", lb=e, add=dx0, name="in_proj_dx")
            G["w_in_ab"][e] = _matmul(sv["x0"], dhin, mode="tn", name="in_proj_dw")
            G["gmlp_ln_g"][e], G["gmlp_ln_b"][e] = dlg[0], dlb[0]
            G["gmlp_w_s"][e], G["gmlp_b_s"][e] = dws, dbs[:, :, 0]
            G["conv_w"][e], G["conv_b"][e], G["conv_gn_g"][e], G["conv_gn_b"][e] = dcw, dcb[0], dgg[0], dgb[0]
        else:
            o = l // 2
            dyT = _matmul(W["w_out_c"], dmix, mode="nt", la=o, out_dtype=BF16, name="sb_out_proj_dx")
            G["w_out_c"][o] = _matmul(sv["yT"].reshape(D, T), dmix, name="sb_out_proj_dw")
            dqT, dkT, dvT = _sb_bwd(sv["qkvT"].reshape(3 * C_HEADS, C_HEAD_DIM, T),
                                    dyT.reshape(C_HEADS, C_HEAD_DIM, T), n_ex, S)
            dqkvT = jnp.concatenate([dqT, dkT, dvT], axis=0).reshape(3 * D, T)
            dh = _matmul(dqkvT, W["w_qkv_cT"], mode="tn", lb=o, add=dx0, name="qkv_proj_dx")
            G["w_qkv_c"][o] = _matmul(dqkvT, sv["x0"], name="qkv_proj_dw").T
        G["ln_g"][l] = jnp.concatenate(lng, axis=0)
        G["ln_b"][l] = jnp.concatenate(lnb, axis=0)
    grads = {k: jnp.stack(v, axis=0) for k, v in G.items()}
    return loss_part, dh.reshape(n_ex, S, D), grads


def _allgather(p, name):
    def body(p_ref, out_ref, send_sems, recv_sems, local_sem):
        x, y, c = (lax.axis_index(a) for a in AXES)
        me, sibling = (x, y, c), (x, y, 1 - c)
        chips = [(1 - x, y), (x, 1 - y), (1 - x, 1 - y)]

        def rows(px, py, pc):
            return out_ref.at[4 * px + 2 * py + pc]

        def copy(k, block, to, src=None):
            return pltpu.make_async_remote_copy(
                src_ref=rows(*block) if src is None else src, dst_ref=rows(*block), send_sem=send_sems.at[k],
                recv_sem=recv_sems.at[k], device_id=to, device_id_type=pl.DeviceIdType.MESH)

        mine = pltpu.make_async_copy(p_ref, rows(*me), local_sem)
        mine.start()
        first = [copy(0, me, sibling, src=p_ref)]
        first += [copy(1 + j, me, (*chip, c), src=p_ref) for j, chip in enumerate(chips)]
        for cp in first:
            cp.start()
        passed = [copy(4 + j, (*chip, c), sibling) for j, chip in enumerate(chips)]
        for j, chip in enumerate(chips):
            copy(1 + j, (*chip, c), me).wait_recv()
            passed[j].start()
        copy(0, sibling, me).wait_recv()
        for j, chip in enumerate(chips):
            copy(4 + j, (*chip, 1 - c), me).wait_recv()
        for cp in first + passed:
            cp.wait_send()
        mine.wait()

    anyspace = pl.BlockSpec(memory_space=pl.ANY)
    return pl.pallas_call(
        body, name=name, in_specs=[anyspace], out_specs=anyspace,
        out_shape=jax.ShapeDtypeStruct((N_DEV,) + p.shape, p.dtype),
        scratch_shapes=[pltpu.SemaphoreType.DMA((N_DEV - 1,)), pltpu.SemaphoreType.DMA((N_DEV - 1,)),
                        pltpu.SemaphoreType.DMA],
    )(p)


N_CHIP = 4
N_CORE = 2


def _pair_swap(p, name):
    def body(p_ref, out_ref, send_sem, recv_sem):
        x, y, c = (lax.axis_index(a) for a in AXES)
        cp = pltpu.make_async_remote_copy(src_ref=p_ref.at[1 - c], dst_ref=out_ref, send_sem=send_sem,
                                          recv_sem=recv_sem, device_id=(x, y, 1 - c),
                                          device_id_type=pl.DeviceIdType.MESH)
        cp.start()
        cp.wait()

    anyspace = pl.BlockSpec(memory_space=pl.ANY)
    return pl.pallas_call(
        body, name=name, in_specs=[anyspace], out_specs=anyspace,
        out_shape=jax.ShapeDtypeStruct(p.shape[1:], p.dtype),
        scratch_shapes=[pltpu.SemaphoreType.DMA, pltpu.SemaphoreType.DMA],
    )(p)


def _pair_add(p, s):
    _, n, R, C = p.shape
    tr = _tile(R, 512, SUBLANE)

    def body(p_ref, s_ref, o_ref):
        o_ref[...] = p_ref[lax.axis_index("c")] + s_ref[...]

    return _call(body, name="pair_add", grid=(n, R // tr),
                 in_specs=[pl.BlockSpec((N_CORE, None, tr, C), lambda k, i: (0, k, i, 0)),
                           pl.BlockSpec((None, tr, C), lambda k, i: (k, i, 0))],
                 out_specs=pl.BlockSpec((None, tr, C), lambda k, i: (k, i, 0)),
                 out_shape=jax.ShapeDtypeStruct(s.shape, s.dtype), sem=("parallel", "parallel"))(p, s)


def _chip_exchange(q, name):
    def body(q_ref, out_ref, send_sems, recv_sems, local_sem):
        x, y, c = (lax.axis_index(a) for a in AXES)
        mine = 2 * x + y
        local = pltpu.make_async_copy(q_ref.at[mine], out_ref.at[mine], local_sem)
        local.start()
        sends, recvs = [], []
        for j in range(1, N_CHIP):
            px = 1 - x if (j >> 1) & 1 else x
            py = 1 - y if j & 1 else y
            theirs = 2 * px + py
            common = dict(send_sem=send_sems.at[j - 1], recv_sem=recv_sems.at[j - 1], device_id=(px, py, c),
                          device_id_type=pl.DeviceIdType.MESH)
            sends.append(pltpu.make_async_remote_copy(src_ref=q_ref.at[theirs], dst_ref=out_ref.at[mine], **common))
            recvs.append(pltpu.make_async_remote_copy(src_ref=q_ref.at[theirs], dst_ref=out_ref.at[theirs], **common))
        for cp in sends:
            cp.start()
        for cp in recvs:
            cp.wait_recv()
        for cp in sends:
            cp.wait_send()
        local.wait()

    anyspace = pl.BlockSpec(memory_space=pl.ANY)
    return pl.pallas_call(
        body, name=name, in_specs=[anyspace], out_specs=anyspace,
        out_shape=jax.ShapeDtypeStruct(q.shape, q.dtype),
        scratch_shapes=[pltpu.SemaphoreType.DMA((N_CHIP - 1,)), pltpu.SemaphoreType.DMA((N_CHIP - 1,)),
                        pltpu.SemaphoreType.DMA],
    )(q)


def _adamw(parts, w, m, v):
    R, C = w.shape
    n_parts = parts.shape[0]
    tr = _tile(R, 128, SUBLANE)
    c1 = 1.0 - ADAM_B1 ** ADAM_STEP
    c2 = 1.0 - ADAM_B2 ** ADAM_STEP

    def body(p_ref, w_ref, m_ref, v_ref, g_ref, d_ref, mo_ref, vo_ref):
        g = p_ref[0]
        for s in range(1, n_parts):
            g = g + p_ref[s]
        mn = ADAM_B1 * m_ref[...] + (1.0 - ADAM_B1) * g
        vn = ADAM_B2 * v_ref[...] + (1.0 - ADAM_B2) * (g * g)
        m_hat = mn / c1
        v_hat = vn / c2
        g_ref[...] = g
        d_ref[...] = -ADAM_LR * (m_hat / (jnp.sqrt(v_hat) + ADAM_EPS) + ADAM_WD * w_ref[...])
        mo_ref[...] = mn
        vo_ref[...] = vn

    row = pl.BlockSpec((tr, C), lambda i: (i, 0))
    out = jax.ShapeDtypeStruct((R, C), F32)
    return _call(body, name="adamw", grid=(R // tr,),
                 in_specs=[pl.BlockSpec((n_parts, tr, C), lambda i: (0, i, 0)), row, row, row],
                 out_specs=[row, row, row, row], out_shape=[out, out, out, out], sem=("parallel",))(parts, w, m, v)


_MATMUL_W = [("w_in_ab", 2), ("w_out_ab", 1), ("w_qkv_c", 2), ("w_out_c", 1), ("mem_wq", 1), ("mem_wk", 1),
             ("mem_wv", 1), ("mem_wo", 1), ("ffn_w1", 2), ("ffn_w3", 2), ("ffn_w2", 1)]
_SMALL_SHARDED = [("conv_w", 2), ("ln_g", 2), ("ln_b", 2)]
_REPLICATED = ["gmlp_ln_g", "gmlp_ln_b", "gmlp_w_s", "gmlp_b_s", "conv_b", "conv_gn_g", "conv_gn_b"]
_WEIGHTS = ["w_in_ab", "gmlp_ln_g", "gmlp_ln_b", "gmlp_w_s", "gmlp_b_s", "conv_w", "conv_b", "conv_gn_g",
            "conv_gn_b", "w_out_ab", "w_qkv_c", "w_out_c", "mem_wq", "mem_wk", "mem_wv", "mem_wo", "ffn_w1",
            "ffn_w3", "ffn_w2", "ln_g", "ln_b"]
_PACK_ORDER = [n for n, _ in _MATMUL_W] + [n for n, _ in _SMALL_SHARDED] + _REPLICATED
_PACK_ROW_MULT = 128


_FF_AXIS = {"ffn_w1": 2, "ffn_w3": 2, "ffn_w2": 1}


def _to_padded(name, a):
    axis = _FF_AXIS.get(name)
    if axis is None:
        return a
    widths = [(0, 0)] * a.ndim
    widths[axis] = (0, D_FF_SHARD_PAD - a.shape[axis])
    return jnp.pad(a, widths)


def _from_padded(name, a):
    axis = _FF_AXIS.get(name)
    return a if axis is None else lax.slice_in_dim(a, 0, D_FF_SHARD, axis=axis)


def _unshard(g, axis):
    full = jnp.moveaxis(g, 0, axis)
    shp = full.shape
    return full.reshape(shp[:axis] + (shp[axis] * shp[axis + 1],) + shp[axis + 2:])


def _split_shards(full, axis):
    shp = full.shape
    parts = full.reshape(shp[:axis] + (N_DEV, shp[axis] // N_DEV) + shp[axis + 1:])
    return jnp.moveaxis(parts, axis, 0)


def _pack_rows(flat_parts, lead=()):
    flat = jnp.concatenate(flat_parts, axis=-1)
    n = flat.shape[-1]
    per = PACK_COLS * _PACK_ROW_MULT
    total = -(-n // per) * per
    if total != n:
        flat = jnp.concatenate([flat, jnp.zeros(lead + (total - n,), flat.dtype)], axis=-1)
    return flat.reshape(lead + (total // PACK_COLS, PACK_COLS))


def _gather_weights(w):
    big = _pack_rows([w[n].astype(BF16).reshape(-1) for n, _ in _MATMUL_W])
    small = _pack_rows([w[n].reshape(-1) for n, _ in _SMALL_SHARDED])
    big_all = _allgather(big, "gather_matmul_weights").reshape(N_DEV, -1)
    small_all = _allgather(small, "gather_small_weights").reshape(N_DEV, -1)
    full = {n: w[n] for n in _REPLICATED}
    for table, src in ((_MATMUL_W, big_all), (_SMALL_SHARDED, small_all)):
        off = 0
        for n, axis in table:
            size = w[n].size
            full[n] = _unshard(src[:, off:off + size].reshape((N_DEV,) + w[n].shape), axis)
            off += size
    full["w_qkv_cT"] = jnp.swapaxes(full.pop("w_qkv_c"), 1, 2)
    full["ffn_w13"] = jnp.concatenate([full.pop("ffn_w1"), full.pop("ffn_w3")], axis=2)
    return full


def _pack_grads(grads):
    pieces = [_split_shards(grads[n], axis).reshape(N_DEV, -1) for n, axis in _MATMUL_W + _SMALL_SHARDED]
    pieces += [jnp.broadcast_to(grads[n].reshape(1, -1), (N_DEV, grads[n].size)) for n in _REPLICATED]
    return _pack_rows(pieces, lead=(N_DEV,))


def _pack_local(w):
    return _pack_rows([w[n].reshape(-1) for n in _PACK_ORDER])


def _unpack_local(packed, like):
    flat = packed.reshape(-1)
    out, off = {}, 0
    for n in _PACK_ORDER:
        out[n] = flat[off:off + like[n].size].reshape(like[n].shape)
        off += like[n].size
    return out


def kernel(x, mem, w_in_ab, gmlp_ln_g, gmlp_ln_b, gmlp_w_s, gmlp_b_s, conv_w, conv_b, conv_gn_g, conv_gn_b, w_out_ab, w_qkv_c, w_out_c, mem_wq, mem_wk, mem_wv, mem_wo, ffn_w1, ffn_w3, ffn_w2, ln_g, ln_b, loss_target, m_w_in_ab, m_gmlp_ln_g, m_gmlp_ln_b, m_gmlp_w_s, m_gmlp_b_s, m_conv_w, m_conv_b, m_conv_gn_g, m_conv_gn_b, m_w_out_ab, m_w_qkv_c, m_w_out_c, m_mem_wq, m_mem_wk, m_mem_wv, m_mem_wo, m_ffn_w1, m_ffn_w3, m_ffn_w2, m_ln_g, m_ln_b, v_w_in_ab, v_gmlp_ln_g, v_gmlp_ln_b, v_gmlp_w_s, v_gmlp_b_s, v_conv_w, v_conv_b, v_conv_gn_g, v_conv_gn_b, v_w_out_ab, v_w_qkv_c, v_w_out_c, v_mem_wq, v_mem_wk, v_mem_wv, v_mem_wo, v_ffn_w1, v_ffn_w3, v_ffn_w2, v_ln_g, v_ln_b):
    w = dict(w_in_ab=w_in_ab, gmlp_ln_g=gmlp_ln_g, gmlp_ln_b=gmlp_ln_b, gmlp_w_s=gmlp_w_s, gmlp_b_s=gmlp_b_s,
             conv_w=conv_w, conv_b=conv_b, conv_gn_g=conv_gn_g, conv_gn_b=conv_gn_b, w_out_ab=w_out_ab,
             w_qkv_c=w_qkv_c, w_out_c=w_out_c, mem_wq=mem_wq, mem_wk=mem_wk, mem_wv=mem_wv, mem_wo=mem_wo,
             ffn_w1=ffn_w1, ffn_w3=ffn_w3, ffn_w2=ffn_w2, ln_g=ln_g, ln_b=ln_b)
    m = dict(w_in_ab=m_w_in_ab, gmlp_ln_g=m_gmlp_ln_g, gmlp_ln_b=m_gmlp_ln_b, gmlp_w_s=m_gmlp_w_s,
             gmlp_b_s=m_gmlp_b_s, conv_w=m_conv_w, conv_b=m_conv_b, conv_gn_g=m_conv_gn_g,
             conv_gn_b=m_conv_gn_b, w_out_ab=m_w_out_ab, w_qkv_c=m_w_qkv_c, w_out_c=m_w_out_c, mem_wq=m_mem_wq,
             mem_wk=m_mem_wk, mem_wv=m_mem_wv, mem_wo=m_mem_wo, ffn_w1=m_ffn_w1, ffn_w3=m_ffn_w3,
             ffn_w2=m_ffn_w2, ln_g=m_ln_g, ln_b=m_ln_b)
    v = dict(w_in_ab=v_w_in_ab, gmlp_ln_g=v_gmlp_ln_g, gmlp_ln_b=v_gmlp_ln_b, gmlp_w_s=v_gmlp_w_s,
             gmlp_b_s=v_gmlp_b_s, conv_w=v_conv_w, conv_b=v_conv_b, conv_gn_g=v_conv_gn_g,
             conv_gn_b=v_conv_gn_b, w_out_ab=v_w_out_ab, w_qkv_c=v_w_qkv_c, w_out_c=v_w_out_c, mem_wq=v_mem_wq,
             mem_wk=v_mem_wk, mem_wv=v_mem_wv, mem_wo=v_mem_wo, ffn_w1=v_ffn_w1, ffn_w3=v_ffn_w3,
             ffn_w2=v_ffn_w2, ln_g=v_ln_g, ln_b=v_ln_b)

    w, m, v = ({n: _to_padded(n, t[n]) for n in _WEIGHTS} for t in (w, m, v))
    full = _gather_weights(w)
    loss_part, grad_x, grads = _local_step(x, mem, loss_target, full)
    loss = lax.psum(jnp.sum(loss_part), AXES)

    by_dev = _pack_grads(grads)
    by_core = jnp.swapaxes(by_dev.reshape((N_CHIP, N_CORE) + by_dev.shape[1:]), 0, 1)
    chip_sum = _pair_add(by_core, _pair_swap(by_core, "swap_grads_in_chip"))
    parts = _chip_exchange(chip_sum, "exchange_grads")
    packed = _adamw(parts, _pack_local(w), _pack_local(m), _pack_local(v))
    g_o, d_o, m_o, v_o = ({n: _from_padded(n, a) for n, a in _unpack_local(t, w).items()} for t in packed)
    return (loss, grad_x, *[g_o[n] for n in _WEIGHTS], *[d_o[n] for n in _WEIGHTS],
            *[m_o[n] for n in _WEIGHTS], *[v_o[n] for n in _WEIGHTS])
```
